```python
import math
import jax, jax.numpy as jnp
from jax import lax
import numpy as np

D_MODEL = 1024
BATCH = 8
SEQ = 2048
DEPTH = 2
DEC_BATCH = 128
DEC_SEQ = 8
PAST_LEN = 2048
PAGE_SIZE = 128

N_META = 16
RW_WIDTH = D_MODEL // 2
RW_HEAD = 64
RW_HEADS = RW_WIDTH // RW_HEAD
DECAY_LORA = 64
ICLR_LORA = 64
GN_EPS = 64e-5
DA_WIDTH = D_MODEL // 2
DA_QK = 64
DA_HEADS = DA_WIDTH // (2 * DA_QK)
DA_V = 2 * DA_QK
N_BUCKETS = 32
MAX_DISTANCE = 128
Q_BLOCK = 128
NORM_EPS = 1e-6
NEG_INF = -1e30

RW_R0 = 0
RW_K0 = RW_R0 + RW_WIDTH
RW_V0 = RW_K0 + RW_WIDTH
RW_WD0 = RW_V0 + RW_WIDTH
RW_AD0 = RW_WD0 + DECAY_LORA
RW_Z0 = RW_AD0 + ICLR_LORA
RW_COLS = RW_Z0 + RW_WIDTH
DA_Q0 = RW_COLS
DA_K0 = DA_Q0 + DA_WIDTH
DA_V0 = DA_K0 + DA_WIDTH
DA_Z0 = DA_V0 + DA_HEADS * DA_V
GA0 = DA_Z0 + DA_WIDTH
GB0 = GA0 + D_MODEL
N_COLS = GB0 + D_MODEL

kernel_name = "rwkv7_diffattn_gated_hybrid_step"


def rms_norm(x, g, eps=NORM_EPS):
    xf = x.astype(jnp.float32)
    y = xf * lax.rsqrt(jnp.mean(xf * xf, axis=-1, keepdims=True) + eps)
    return (y * g.astype(jnp.float32)).astype(x.dtype)


def lambda_init(layer):
    return 0.8 - 0.6 * math.exp(-0.3 * layer)


def t5_bucket(q_pos, k_pos):
    n = jnp.maximum(q_pos[:, None] - k_pos[None, :], 0)
    max_exact = N_BUCKETS // 2
    nf = jnp.maximum(n, 1).astype(jnp.float32)
    large = max_exact + (jnp.log(nf / max_exact) / math.log(MAX_DISTANCE / max_exact)
                         * (N_BUCKETS - max_exact)).astype(jnp.int32)
    large = jnp.minimum(large, N_BUCKETS - 1)
    return jnp.where(n < max_exact, n, large)


def diff_attend(q, k, v, q_pos, k_pos, rel_bias, lam):
    s = jnp.einsum('bqhmd,bkhmd->bhmqk', q, k).astype(jnp.float32) * (DA_QK ** -0.5)
    bias = rel_bias[t5_bucket(q_pos, k_pos)].astype(jnp.float32)
    s = s + jnp.transpose(bias, (2, 0, 1))[None, :, None]
    mask = k_pos[None, :] <= q_pos[:, None]
    s = jnp.where(mask, s, NEG_INF)
    p = jax.nn.softmax(s, axis=-1)
    attn = p[:, :, 0] - lam * p[:, :, 1]
    return jnp.einsum('bhqk,bkhe->bqhe', attn.astype(v.dtype), v)


def diff_attend_blocked(q, k, v, q_pos, k_pos, rel_bias, lam):
    B, Tq = q.shape[0], q.shape[1]
    if Tq <= Q_BLOCK:
        return diff_attend(q, k, v, q_pos, k_pos, rel_bias, lam)
    nb = -(-Tq // Q_BLOCK)
    pad = nb * Q_BLOCK - Tq
    qp = jnp.pad(q, ((0, 0), (0, pad), (0, 0), (0, 0), (0, 0)))
    pp = jnp.pad(q_pos, (0, pad), mode='edge')
    qb = jnp.moveaxis(qp.reshape(B, nb, Q_BLOCK, DA_HEADS, 2, DA_QK), 1, 0)
    pb = pp.reshape(nb, Q_BLOCK)
    out = lax.map(lambda a: diff_attend(a[0], k, v, a[1], k_pos, rel_bias, lam), (qb, pb))
    out = jnp.moveaxis(out, 0, 1).reshape(B, nb * Q_BLOCK, DA_HEADS, DA_V)
    return out[:, :Tq]


def rwkv7_branch(h, s0, w0, w2, a0, a2, k_k, k_a, r_k, lnx_g, lnx_b):
    f32 = jnp.float32
    B, T, _ = h.shape
    r = h[..., RW_R0:RW_K0].astype(f32)
    k = h[..., RW_K0:RW_V0].astype(f32)
    v = h[..., RW_V0:RW_WD0].astype(f32)
    wd = h[..., RW_WD0:RW_AD0]
    ad = h[..., RW_AD0:RW_Z0]
    z = h[..., RW_Z0:RW_COLS].astype(f32)
    w_log = -jax.nn.softplus(-(w0 + jnp.tanh(wd) @ w2).astype(f32)) - 0.5
    decay = jnp.exp(-jnp.exp(w_log))
    a = jax.nn.sigmoid((a0 + ad @ a2).astype(f32))
    hd = lambda t: t.reshape(B, T, RW_HEADS, RW_HEAD)
    kk = hd(k * k_k.astype(f32))
    kk = kk / jnp.maximum(jnp.sqrt(jnp.sum(kk * kk, axis=-1, keepdims=True)), 1e-12)
    k = k * (1.0 + (a - 1.0) * k_a.astype(f32))
    r_, w_, k_, v_, a_ = hd(r), hd(decay), hd(k), hd(v), hd(a)
    rem = -kk
    rep = kk * a_

    def step(S, inp):
        rt, wt, kt, vt, at, bt = inp
        sa = jnp.einsum('bhij,bhj->bhi', S, at)
        S = S * wt[:, :, None, :] + sa[..., None] * bt[:, :, None, :] + vt[..., None] * kt[:, :, None, :]
        return S, jnp.einsum('bhij,bhj->bhi', S, rt)

    tm = lambda t: jnp.swapaxes(t, 0, 1)
    S, y = lax.scan(step, s0.astype(f32), (tm(r_), tm(w_), tm(k_), tm(v_), tm(rem), tm(rep)))
    y = tm(y)
    mean = jnp.mean(y, axis=-1, keepdims=True)
    var = jnp.mean(jnp.square(y - mean), axis=-1, keepdims=True)
    y = ((y - mean) * lax.rsqrt(var + GN_EPS)).reshape(B, T, RW_WIDTH)
    y = y * lnx_g.astype(f32) + lnx_b.astype(f32)
    bonus = jnp.sum(r_ * k_ * r_k.astype(f32), axis=-1, keepdims=True) * v_
    y = (y + bonus.reshape(B, T, RW_WIDTH)) * jax.nn.silu(z)
    return y.astype(h.dtype), S


def trunk(x, pos, shift0, state0, cache_k, cache_v, page_table, rel_bias, norm_g, w_in, mu_shift,
          w0, w2, a0, a2, k_k, k_a, r_k, lnx_g, lnx_b, q_norm_g, k_norm_g,
          lam_q1, lam_k1, lam_q2, lam_k2, subln_g, w_a_out, w_b_out, w_o):
    B, T, _ = x.shape
    new_k, new_v, new_s, new_shift = [], [], [], []
    for l in range(DEPTH):
        xn = rms_norm(x, norm_g[l])
        proj = xn @ w_in[l]
        p_rw = proj[..., :RW_COLS]
        if shift0 is None:
            p_prev = jnp.zeros((B, RW_COLS), p_rw.dtype)
            s_init = jnp.zeros((B, RW_HEADS, RW_HEAD, RW_HEAD), jnp.float32)
        else:
            p_prev = shift0[l].astype(xn.dtype) @ w_in[l][:, :RW_COLS]
            s_init = state0[l]
        shifted = jnp.concatenate([p_prev[:, None], p_rw[:, :-1]], axis=1)
        h = p_rw + (shifted - p_rw) * mu_shift[l]
        y_a, S = rwkv7_branch(h, s_init, w0[l], w2[l], a0[l], a2[l], k_k[l], k_a[l], r_k[l],
                              lnx_g[l], lnx_b[l])
        q = rms_norm(proj[..., DA_Q0:DA_K0].reshape(B, T, DA_HEADS, 2, DA_QK), q_norm_g[l])
        k = rms_norm(proj[..., DA_K0:DA_V0].reshape(B, T, DA_HEADS, 2, DA_QK), k_norm_g[l])
        v = proj[..., DA_V0:DA_Z0].reshape(B, T, DA_HEADS, DA_V)
        if cache_k is None:
            kf, vf, k_pos = k, v, pos
        else:
            past = page_table.shape[1] * PAGE_SIZE
            kp = cache_k[l, page_table].reshape(B, past, DA_HEADS, 2, DA_QK).astype(k.dtype)
            vp = cache_v[l, page_table].reshape(B, past, DA_HEADS, DA_V).astype(v.dtype)
            kf = jnp.concatenate([kp, k], axis=1)
            vf = jnp.concatenate([vp, v], axis=1)
            k_pos = jnp.arange(past + T)
        f32 = jnp.float32
        li = lambda_init(l)
        lam = (jnp.exp(jnp.sum(lam_q1[l].astype(f32) * lam_k1[l].astype(f32)))
               - jnp.exp(jnp.sum(lam_q2[l].astype(f32) * lam_k2[l].astype(f32))) + li)
        o = diff_attend_blocked(q, kf, vf, pos, k_pos, rel_bias, lam)
        o = rms_norm(o, subln_g[l]) * (1.0 - li)
        y_b = o.reshape(B, T, DA_HEADS * DA_V) * jax.nn.silu(proj[..., DA_Z0:GA0])
        m = (jax.nn.sigmoid(proj[..., GA0:GB0]) * (y_a @ w_a_out[l])
             + jax.nn.sigmoid(proj[..., GB0:N_COLS]) * (y_b @ w_b_out[l]))
        x = x + m @ w_o[l]
        new_k.append(k.reshape(B, T, DA_HEADS, 2 * DA_QK))
        new_v.append(v)
        new_s.append(S.astype(x.dtype))
        new_shift.append(xn[:, -1])
    return x, jnp.stack(new_k), jnp.stack(new_v), jnp.stack(new_s), jnp.stack(new_shift)


def setup_inputs(seed: int = 0) -> dict:
    key = jax.random.key(seed)
    ks = iter(jax.random.split(key, 40))
    f32 = jnp.float32
    nrm = lambda shape, s=1.0: jax.random.normal(next(ks), shape, f32) * s
    n_pages = PAST_LEN // PAGE_SIZE
    used = DEC_BATCH * n_pages
    n_pool = used + max(used // 4, 1)
    page_table = jax.random.permutation(next(ks), n_pool)[:used].reshape(DEC_BATCH, n_pages).astype(jnp.int32)
    return {
        "x_prompt": nrm((BATCH, SEQ, D_MODEL)),
        "x_sample": nrm((DEC_BATCH, DEC_SEQ, D_MODEL)),
        "cache_k": nrm((DEPTH, n_pool, PAGE_SIZE, DA_HEADS, 2 * DA_QK)),
        "cache_v": nrm((DEPTH, n_pool, PAGE_SIZE, DA_HEADS, DA_V)),
        "page_table": page_table,
        "state_rwkv": nrm((DEPTH, DEC_BATCH, RW_HEADS, RW_HEAD, RW_HEAD), 0.3),
        "state_shift": nrm((DEPTH, DEC_BATCH, D_MODEL)),
        "meta_tokens": nrm((N_META, D_MODEL)),
        "rel_bias": nrm((N_BUCKETS, DA_HEADS), 0.3),
        "norm_g": 1.0 + nrm((DEPTH, D_MODEL), 0.05),
        "w_in": nrm((DEPTH, D_MODEL, N_COLS), D_MODEL ** -0.5),
        "mu_shift": jax.random.uniform(next(ks), (DEPTH, RW_COLS), f32),
        "w0": jax.random.uniform(next(ks), (DEPTH, RW_WIDTH), f32, -5.0, 1.0),
        "w2": nrm((DEPTH, DECAY_LORA, RW_WIDTH), 0.5 * DECAY_LORA ** -0.5),
        "a0": nrm((DEPTH, RW_WIDTH), 0.5),
        "a2": nrm((DEPTH, ICLR_LORA, RW_WIDTH), 0.5 * ICLR_LORA ** -0.5),
        "k_k": 0.85 + nrm((DEPTH, RW_WIDTH), 0.05),
        "k_a": 1.0 + nrm((DEPTH, RW_WIDTH), 0.05),
        "r_k": nrm((DEPTH, RW_HEADS, RW_HEAD), 0.1),
        "lnx_g": 1.0 + nrm((DEPTH, RW_WIDTH), 0.05),
        "lnx_b": nrm((DEPTH, RW_WIDTH), 0.02),
        "q_norm_g": 1.0 + nrm((DEPTH, DA_QK), 0.05),
        "k_norm_g": 1.0 + nrm((DEPTH, DA_QK), 0.05),
        "lam_q1": nrm((DEPTH, DA_QK), 0.1),
        "lam_k1": nrm((DEPTH, DA_QK), 0.1),
        "lam_q2": nrm((DEPTH, DA_QK), 0.1),
        "lam_k2": nrm((DEPTH, DA_QK), 0.1),
        "subln_g": 1.0 + nrm((DEPTH, DA_V), 0.05),
        "w_a_out": nrm((DEPTH, RW_WIDTH, D_MODEL), RW_WIDTH ** -0.5),
        "w_b_out": nrm((DEPTH, DA_HEADS * DA_V, D_MODEL), (DA_HEADS * DA_V) ** -0.5),
        "w_o": nrm((DEPTH, D_MODEL, D_MODEL), D_MODEL ** -0.5),
    }


def reference(x_prompt, x_sample, cache_k, cache_v, page_table, state_rwkv, state_shift,
              meta_tokens, rel_bias, norm_g, w_in, mu_shift, w0, w2, a0, a2, k_k, k_a, r_k,
              lnx_g, lnx_b, q_norm_g, k_norm_g, lam_q1, lam_k1, lam_q2, lam_k2, subln_g,
              w_a_out, w_b_out, w_o):
    B = x_prompt.shape[0]
    meta = jnp.broadcast_to(meta_tokens[None].astype(x_prompt.dtype), (B, N_META, D_MODEL))
    xp = jnp.concatenate([meta, x_prompt], axis=1)
    pos_p = jnp.arange(xp.shape[1])
    yp, kp, vp, sp, hp = trunk(xp, pos_p, None, None, None, None, None, rel_bias, norm_g, w_in,
                               mu_shift, w0, w2, a0, a2, k_k, k_a, r_k, lnx_g, lnx_b, q_norm_g,
                               k_norm_g, lam_q1, lam_k1, lam_q2, lam_k2, subln_g, w_a_out,
                               w_b_out, w_o)
    past = page_table.shape[1] * PAGE_SIZE
    pos_s = past + jnp.arange(x_sample.shape[1])
    ys, k_s, v_s, s_s, h_s = trunk(x_sample, pos_s, state_shift, state_rwkv, cache_k, cache_v,
                                   page_table, rel_bias, norm_g, w_in, mu_shift, w0, w2, a0, a2,
                                   k_k, k_a, r_k, lnx_g, lnx_b, q_norm_g, k_norm_g, lam_q1,
                                   lam_k1, lam_q2, lam_k2, subln_g, w_a_out, w_b_out, w_o)
    return (yp[:, N_META:], ys, kp, vp, sp, hp, k_s, v_s, s_s, h_s)
```

```python
import functools
import math

import jax
import jax.numpy as jnp
from jax import lax
from jax.experimental import pallas as pl
from jax.experimental.pallas import tpu as pltpu

F32 = jnp.float32
BF16 = jnp.bfloat16

D_MODEL = 1024
N_META = 16
PAGE_SIZE = 128
RW_WIDTH = 512
RW_HEAD = 64
RW_HEADS = 8
LORA = 64
GN_EPS = 64e-5
DA_WIDTH = 512
DA_QK = 64
DA_HEADS = 4
DA_V = 128
N_BUCKETS = 32
MAX_DISTANCE = 128
NORM_EPS = 1e-6
NEG_INF = -1e30

RW_COLS = 3 * RW_WIDTH + 2 * LORA + RW_WIDTH
DA_COLS = 4 * DA_WIDTH
G_COLS = 2 * D_MODEL

SEQ_TILE = 128
VMEM_LIMIT = 48 * 1024 * 1024


def _cparams(sem):
    return pltpu.CompilerParams(dimension_semantics=sem, vmem_limit_bytes=VMEM_LIMIT)


def _head_sum(x, e_ref):
    hi = x.astype(BF16)
    lo = (x - hi.astype(F32)).astype(BF16)
    e = e_ref[...]
    return (jnp.dot(hi, e, preferred_element_type=F32)
            + jnp.dot(lo, e, preferred_element_type=F32))


def _in_proj_body(x_ref, g_ref, w_ref, o_ref, *, normalize):
    x = x_ref[...]
    if normalize:
        ms = jnp.mean(x * x, axis=-1, keepdims=True)
        x = x * lax.rsqrt(ms + NORM_EPS) * g_ref[...]
    o_ref[...] = jnp.dot(x.astype(BF16), w_ref[...], preferred_element_type=F32)


def _in_proj(x2d, g, w, tm, normalize=True):
    n, d = x2d.shape
    c = w.shape[1]
    return pl.pallas_call(
        functools.partial(_in_proj_body, normalize=normalize),
        out_shape=jax.ShapeDtypeStruct((n, c), F32),
        grid=(n // tm,),
        in_specs=[pl.BlockSpec((tm, d), lambda i: (i, 0)),
                  pl.BlockSpec((1, d), lambda i: (0, 0)),
                  pl.BlockSpec((d, c), lambda i: (0, 0))],
        out_specs=pl.BlockSpec((tm, c), lambda i: (i, 0)),
        compiler_params=_cparams(("arbitrary",)),
        name="in_proj",
    )(x2d, g.reshape(1, d), w)


def _norm_rows_body(x_ref, g_ref, o_ref):
    x = x_ref[...]
    ms = jnp.mean(x * x, axis=-1, keepdims=True)
    o_ref[...] = x * lax.rsqrt(ms + NORM_EPS) * g_ref[...]


def _norm_rows(x2d, g):
    n, d = x2d.shape
    return pl.pallas_call(
        _norm_rows_body,
        out_shape=jax.ShapeDtypeStruct((n, d), F32),
        name="norm_rows",
    )(x2d, g.reshape(1, d))


def _softplus(x):
    return jnp.maximum(x, 0.0) + jnp.log1p(jnp.exp(-jnp.abs(x)))


def _rwkv_prep_body(p_ref, prev_ref, mu_ref, w0_ref, w2_ref, a0_ref, a2_ref, kk_ref, ka_ref,
                    rk_ref, e_ref, r_ref, w_ref, k_ref, v_ref, rem_ref, rep_ref, sz_ref, bz_ref,
                    carry_ref, *, tt):
    @pl.when(pl.program_id(1) == 0)
    def _():
        carry_ref[...] = prev_ref[...]

    p = p_ref[...]
    rolled = pltpu.roll(p, 1, axis=0)
    row = lax.broadcasted_iota(jnp.int32, p.shape, 0)
    shifted = jnp.where(row == 0, carry_ref[...], rolled)
    carry_ref[...] = p[tt - 1:tt, :]
    h = p + (shifted - p) * mu_ref[...]

    r = h[:, 0:512]
    k = h[:, 512:1024]
    v = h[:, 1024:1536]
    wd = h[:, 1536:1600]
    ad = h[:, 1600:1664]
    z = h[:, 1664:2176]

    lw = w0_ref[...] + jnp.dot(jnp.tanh(wd).astype(BF16), w2_ref[...], preferred_element_type=F32)
    w_log = -_softplus(-lw) - 0.5
    decay = jnp.exp(-jnp.exp(w_log))
    a = jax.nn.sigmoid(a0_ref[...] + jnp.dot(ad.astype(BF16), a2_ref[...],
                                              preferred_element_type=F32))
    kk = k * kk_ref[...]
    ss = _head_sum(kk * kk, e_ref)
    kk = kk / jnp.maximum(jnp.sqrt(ss), 1e-12)
    kmod = k * (1.0 + (a - 1.0) * ka_ref[...])
    bonus = _head_sum(r * kmod * rk_ref[...], e_ref) * v
    sz = z * jax.nn.sigmoid(z)

    r_ref[...] = r
    w_ref[...] = decay
    k_ref[...] = kmod
    v_ref[...] = v
    rem_ref[...] = -kk
    rep_ref[...] = kk * a
    sz_ref[...] = sz
    bz_ref[...] = bonus * sz


def _rwkv_prep(p3, prev, mu, w0, w2, a0, a2, k_k, k_a, r_k, e, tt):
    b, t, _ = p3.shape
    row = lambda a: a.reshape(1, -1)
    seq_spec = pl.BlockSpec((None, tt, RW_WIDTH), lambda i, j: (i, j, 0))
    full = lambda shape: pl.BlockSpec(shape, lambda i, j: (0,) * len(shape))
    out = jax.ShapeDtypeStruct((b, t, RW_WIDTH), F32)
    return pl.pallas_call(
        functools.partial(_rwkv_prep_body, tt=tt),
        out_shape=(out,) * 8,
        grid=(b, t // tt),
        in_specs=[pl.BlockSpec((None, tt, RW_COLS), lambda i, j: (i, j, 0)),
                  pl.BlockSpec((None, 1, RW_COLS), lambda i, j: (i, 0, 0)),
                  full((1, RW_COLS)), full((1, RW_WIDTH)), full((LORA, RW_WIDTH)),
                  full((1, RW_WIDTH)), full((LORA, RW_WIDTH)), full((1, RW_WIDTH)),
                  full((1, RW_WIDTH)), full((1, RW_WIDTH)), full((RW_WIDTH, RW_WIDTH))],
        out_specs=(seq_spec,) * 8,
        scratch_shapes=[pltpu.VMEM((1, RW_COLS), F32)],
        compiler_params=_cparams(("arbitrary", "arbitrary")),
        name="rwkv_prep",
    )(p3, prev, row(mu), row(w0), w2.astype(BF16), row(a0), a2.astype(BF16), row(k_k), row(k_a),
      row(r_k), e)


def _rwkv_scan_body(r_ref, w_ref, k_ref, v_ref, rem_ref, rep_ref, s0_ref, y_ref, st_ref, s_ref,
                    *, tt, t_valid):
    tb = pl.program_id(1)

    @pl.when(tb == 0)
    def _():
        s_ref[...] = s0_ref[...]

    n = jnp.minimum(tt, t_valid - tb * tt)

    @pl.when(n < tt)
    def _():
        y_ref[...] = jnp.zeros_like(y_ref)

    eye = (lax.broadcasted_iota(jnp.int32, (RW_HEAD, RW_HEAD), 0)
           == lax.broadcasted_iota(jnp.int32, (RW_HEAD, RW_HEAD), 1))

    def step(t, carry):
        rr = r_ref[pl.ds(t, 1), :]
        ww = w_ref[pl.ds(t, 1), :]
        kr = k_ref[pl.ds(t, 1), :]
        vr = v_ref[pl.ds(t, 1), :]
        ar = rem_ref[pl.ds(t, 1), :]
        br = rep_ref[pl.ds(t, 1), :]
        ys = []
        for h in range(RW_HEADS):
            sl = slice(h * RW_HEAD, (h + 1) * RW_HEAD)
            s = s_ref[h]
            v_col = jnp.sum(jnp.where(eye, vr[:, sl], 0.0), axis=1, keepdims=True)
            sa = jnp.sum(s * ar[:, sl], axis=1, keepdims=True)
            s = s * ww[:, sl] + sa * br[:, sl] + v_col * kr[:, sl]
            s_ref[h] = s
            y_col = jnp.sum(s * rr[:, sl], axis=1, keepdims=True)
            ys.append(jnp.sum(jnp.where(eye, y_col, 0.0), axis=0, keepdims=True))
        y_ref[pl.ds(t, 1), :] = jnp.concatenate(ys, axis=1)
        return carry

    lax.fori_loop(0, n, step, 0)

    @pl.when(tb == pl.num_programs(1) - 1)
    def _():
        st_ref[...] = s_ref[...]


def _rwkv_scan(r, w, k, v, rem, rep, s0, tt, t_valid):
    b, t, _ = r.shape
    seq_spec = pl.BlockSpec((None, tt, RW_WIDTH), lambda i, j: (i, j, 0))
    st_spec = pl.BlockSpec((None, RW_HEADS, RW_HEAD, RW_HEAD), lambda i, j: (i, 0, 0, 0))
    return pl.pallas_call(
        functools.partial(_rwkv_scan_body, tt=tt, t_valid=t_valid),
        out_shape=(jax.ShapeDtypeStruct((b, t, RW_WIDTH), F32),
                   jax.ShapeDtypeStruct((b, RW_HEADS, RW_HEAD, RW_HEAD), F32)),
        grid=(b, t // tt),
        in_specs=[seq_spec] * 6 + [st_spec],
        out_specs=(seq_spec, st_spec),
        scratch_shapes=[pltpu.VMEM((RW_HEADS, RW_HEAD, RW_HEAD), F32)],
        compiler_params=_cparams(("arbitrary", "arbitrary")),
        name="rwkv_scan",
    )(r, w, k, v, rem, rep, s0)


def _da_prep_body(p_ref, qg_ref, kg_ref, e_ref, q_ref, k_ref):
    q = p_ref[:, 0:512]
    k = p_ref[:, 512:1024]
    qm = _head_sum(q * q, e_ref) * (1.0 / DA_QK)
    km = _head_sum(k * k, e_ref) * (1.0 / DA_QK)
    q_ref[...] = (q * lax.rsqrt(qm + NORM_EPS) * qg_ref[...]).astype(BF16)
    k_ref[...] = k * lax.rsqrt(km + NORM_EPS) * kg_ref[...]


def _da_prep(pda, qg, kg, e, tm):
    n = pda.shape[0]
    tile8 = lambda g: jnp.tile(g, DA_WIDTH // DA_QK).reshape(1, DA_WIDTH)
    return pl.pallas_call(
        _da_prep_body,
        out_shape=(jax.ShapeDtypeStruct((n, DA_WIDTH), BF16),
                   jax.ShapeDtypeStruct((n, DA_WIDTH), F32)),
        grid=(n // tm,),
        in_specs=[pl.BlockSpec((tm, DA_COLS), lambda i: (i, 0)),
                  pl.BlockSpec((1, DA_WIDTH), lambda i: (0, 0)),
                  pl.BlockSpec((1, DA_WIDTH), lambda i: (0, 0)),
                  pl.BlockSpec((RW_WIDTH, RW_WIDTH), lambda i: (0, 0))],
        out_specs=(pl.BlockSpec((tm, DA_WIDTH), lambda i: (i, 0)),
                   pl.BlockSpec((tm, DA_WIDTH), lambda i: (i, 0))),
        compiler_params=_cparams(("arbitrary",)),
        name="da_prep",
    )(pda, tile8(qg), tile8(kg), e)


def _bias_tiles(n, rel_ref):
    max_exact = N_BUCKETS // 2
    nc = jnp.maximum(n, 0)
    nf = jnp.maximum(nc, 1).astype(F32)
    large = max_exact + (jnp.log(nf / max_exact) / math.log(MAX_DISTANCE / max_exact)
                         * (N_BUCKETS - max_exact)).astype(jnp.int32)
    large = jnp.minimum(large, N_BUCKETS - 1)
    bucket = jnp.where(nc < max_exact, nc, large)
    tiles = []
    for h in range(DA_HEADS):
        acc = jnp.zeros(n.shape, F32)
        for b in range(N_BUCKETS):
            acc = jnp.where(bucket == b, rel_ref[b, h], acc)
        tiles.append(jnp.where(n >= 0, acc, NEG_INF))
    return tiles


def _lambda(lam_ref, layer):
    lq1 = lam_ref[0:1, :]
    lk1 = lam_ref[1:2, :]
    lq2 = lam_ref[2:3, :]
    lk2 = lam_ref[3:4, :]
    li = 0.8 - 0.6 * math.exp(-0.3 * layer)
    lam = (jnp.exp(jnp.sum(lq1 * lk1, axis=1, keepdims=True))
           - jnp.exp(jnp.sum(lq2 * lk2, axis=1, keepdims=True)) + li)
    return lam, li


def _softmax_update(hm, s, vb, m_ref, l_ref, acc_ref):
    m_old = m_ref[hm]
    m_new = jnp.maximum(m_old, jnp.max(s, axis=1, keepdims=True))
    alpha = jnp.exp(m_old - m_new)
    p = jnp.exp(s - m_new)
    l_ref[hm] = alpha * l_ref[hm] + jnp.sum(p, axis=1, keepdims=True)
    acc_ref[hm] = alpha * acc_ref[hm] + jnp.dot(p.astype(BF16), vb, preferred_element_type=F32)
    m_ref[hm] = m_new


def _attn_finish(z, lam_ref, sg_ref, m_ref, l_ref, acc_ref, layer):
    lam, li = _lambda(lam_ref, layer)
    outs = []
    for h in range(DA_HEADS):
        o = acc_ref[2 * h] / l_ref[2 * h] - lam * (acc_ref[2 * h + 1] / l_ref[2 * h + 1])
        ms = jnp.mean(o * o, axis=1, keepdims=True)
        outs.append(o * lax.rsqrt(ms + NORM_EPS) * sg_ref[...] * (1.0 - li))
    return jnp.concatenate(outs, axis=1) * (z * jax.nn.sigmoid(z))


def _split_maps(qh):
    lane = lax.broadcasted_iota(jnp.int32, qh.shape, 1)
    zero = jnp.zeros_like(qh)
    return jnp.where(lane < DA_QK, qh, zero), jnp.where(lane >= DA_QK, qh, zero)


_QK_DIMS = (((1,), (1,)), ((), ()))


def _attn_prompt_body(q_ref, k_ref, v_ref, z_ref, rel_ref, lam_ref, sg_ref, o_ref,
                      m_ref, l_ref, acc_ref, bias_ref, *, tq, layer):
    b = pl.program_id(0)
    qi = pl.program_id(1)
    ki = pl.program_id(2)

    @pl.when((b == 0) & (qi == 0) & (ki == 0))
    def _():
        d = (lax.broadcasted_iota(jnp.int32, (tq, tq), 0)
             - lax.broadcasted_iota(jnp.int32, (tq, tq), 1))
        for blk in range(2):
            tiles = _bias_tiles(d + blk * tq, rel_ref)
            for h in range(DA_HEADS):
                bias_ref[blk * DA_HEADS + h] = tiles[h]

    @pl.when(ki == 0)
    def _():
        m_ref[...] = jnp.full(m_ref.shape, NEG_INF, F32)
        l_ref[...] = jnp.zeros_like(l_ref)
        acc_ref[...] = jnp.zeros_like(acc_ref)

    @pl.when(ki <= qi)
    def _():
        dblk = qi - ki
        for h in range(DA_HEADS):
            sl = slice(h * DA_V, (h + 1) * DA_V)
            kb = k_ref[:, sl].astype(BF16)
            vb = v_ref[:, sl].astype(BF16)
            far = jnp.full((tq, tq), rel_ref[N_BUCKETS - 1, h], F32)
            bias = jnp.where(dblk == 0, bias_ref[h],
                             jnp.where(dblk == 1, bias_ref[DA_HEADS + h], far))
            for mi, qm in enumerate(_split_maps(q_ref[:, sl])):
                s = lax.dot_general(qm, kb, _QK_DIMS, preferred_element_type=F32)
                s = s * (DA_QK ** -0.5) + bias
                _softmax_update(2 * h + mi, s, vb, m_ref, l_ref, acc_ref)

    @pl.when(ki == qi)
    def _():
        o_ref[...] = _attn_finish(z_ref[...], lam_ref, sg_ref, m_ref, l_ref, acc_ref, layer)


def _attn_prompt(qn, kn, pda, rel_bias, lam, sg, layer, tq):
    b, t, _ = qn.shape
    nq = t // tq
    return pl.pallas_call(
        functools.partial(_attn_prompt_body, tq=tq, layer=layer),
        out_shape=jax.ShapeDtypeStruct((b, t, DA_WIDTH), F32),
        grid=(b, nq, nq),
        in_specs=[pl.BlockSpec((None, tq, DA_WIDTH), lambda i, q, k: (i, q, 0)),
                  pl.BlockSpec((None, tq, DA_WIDTH), lambda i, q, k: (i, jnp.minimum(k, q), 0)),
                  pl.BlockSpec((None, tq, DA_WIDTH), lambda i, q, k: (i, jnp.minimum(k, q), 2)),
                  pl.BlockSpec((None, tq, DA_WIDTH), lambda i, q, k: (i, q, 3)),
                  pl.BlockSpec(memory_space=pltpu.SMEM),
                  pl.BlockSpec((4, DA_QK), lambda i, q, k: (0, 0)),
                  pl.BlockSpec((1, DA_V), lambda i, q, k: (0, 0))],
        out_specs=pl.BlockSpec((None, tq, DA_WIDTH), lambda i, q, k: (i, q, 0)),
        scratch_shapes=[pltpu.VMEM((2 * DA_HEADS, tq, 1), F32),
                        pltpu.VMEM((2 * DA_HEADS, tq, 1), F32),
                        pltpu.VMEM((2 * DA_HEADS, tq, DA_V), F32),
                        pltpu.VMEM((2 * DA_HEADS, tq, tq), F32)],
        compiler_params=_cparams(("arbitrary", "arbitrary", "arbitrary")),
        name="attn_prompt",
    )(qn, kn, pda, pda, rel_bias, lam, sg.reshape(1, DA_V))


def _attn_sample_body(pt_ref, q_ref, kc_ref, vc_ref, kn_ref, vn_ref, z_ref, rel_ref, lam_ref,
                      sg_ref, o_ref, m_ref, l_ref, acc_ref, *, tq, past, layer):
    j = pl.program_id(1)
    npg = pl.num_programs(1)

    @pl.when(j == 0)
    def _():
        m_ref[...] = jnp.full(m_ref.shape, NEG_INF, F32)
        l_ref[...] = jnp.zeros_like(l_ref)
        acc_ref[...] = jnp.zeros_like(acc_ref)

    def attend(k_blk, v_blk, dist):
        tiles = _bias_tiles(dist, rel_ref)
        for h in range(DA_HEADS):
            sl = slice(h * DA_V, (h + 1) * DA_V)
            kb = k_blk[:, sl].astype(BF16)
            vb = v_blk[:, sl].astype(BF16)
            for mi, qm in enumerate(_split_maps(q_ref[:, sl])):
                s = lax.dot_general(qm, kb, _QK_DIMS, preferred_element_type=F32)
                s = s * (DA_QK ** -0.5) + tiles[h]
                _softmax_update(2 * h + mi, s, vb, m_ref, l_ref, acc_ref)

    qpos = past + lax.broadcasted_iota(jnp.int32, (tq, PAGE_SIZE), 0)
    kpos = j * PAGE_SIZE + lax.broadcasted_iota(jnp.int32, (tq, PAGE_SIZE), 1)
    attend(kc_ref[...], vc_ref[...], qpos - kpos)

    @pl.when(j == npg - 1)
    def _():
        d = (lax.broadcasted_iota(jnp.int32, (tq, tq), 0)
             - lax.broadcasted_iota(jnp.int32, (tq, tq), 1))
        attend(kn_ref[...], vn_ref[...], d)
        o_ref[...] = _attn_finish(z_ref[...], lam_ref, sg_ref, m_ref, l_ref, acc_ref, layer)


def _attn_sample(qn, kn, pda, cache_k, cache_v, page_table, rel_bias, lam, sg, layer):
    b, tq, _ = qn.shape
    npg = page_table.shape[1]
    n_pool = cache_k.shape[1]
    ck = cache_k.reshape(cache_k.shape[0], n_pool, PAGE_SIZE, DA_WIDTH)
    cv = cache_v.reshape(cache_v.shape[0], n_pool, PAGE_SIZE, DA_WIDTH)
    page_spec = pl.BlockSpec((None, None, PAGE_SIZE, DA_WIDTH),
                             lambda i, j, pt: (layer, pt[i, j], 0, 0))
    grid_spec = pltpu.PrefetchScalarGridSpec(
        num_scalar_prefetch=1,
        grid=(b, npg),
        in_specs=[pl.BlockSpec((None, tq, DA_WIDTH), lambda i, j, pt: (i, 0, 0)),
                  page_spec, page_spec,
                  pl.BlockSpec((None, tq, DA_WIDTH), lambda i, j, pt: (i, 0, 0)),
                  pl.BlockSpec((None, tq, DA_WIDTH), lambda i, j, pt: (i, 0, 2)),
                  pl.BlockSpec((None, tq, DA_WIDTH), lambda i, j, pt: (i, 0, 3)),
                  pl.BlockSpec(memory_space=pltpu.SMEM),
                  pl.BlockSpec((4, DA_QK), lambda i, j, pt: (0, 0)),
                  pl.BlockSpec((1, DA_V), lambda i, j, pt: (0, 0))],
        out_specs=pl.BlockSpec((None, tq, DA_WIDTH), lambda i, j, pt: (i, 0, 0)),
        scratch_shapes=[pltpu.VMEM((2 * DA_HEADS, tq, 1), F32),
                        pltpu.VMEM((2 * DA_HEADS, tq, 1), F32),
                        pltpu.VMEM((2 * DA_HEADS, tq, DA_V), F32)],
    )
    return pl.pallas_call(
        functools.partial(_attn_sample_body, tq=tq, past=npg * PAGE_SIZE, layer=layer),
        out_shape=jax.ShapeDtypeStruct((b, tq, DA_WIDTH), F32),
        grid_spec=grid_spec,
        compiler_params=_cparams(("arbitrary", "arbitrary")),
        name="attn_sample",
    )(page_table, qn, ck, cv, kn, pda, pda, rel_bias, lam, sg.reshape(1, DA_V))


def _out_proj_body(x_ref, y_ref, sz_ref, bz_ref, yb_ref, g_ref, lg_ref, lb_ref, e_ref,
                   wa_ref, wb_ref, wo_ref, o_ref):
    y = y_ref[...]
    mean = _head_sum(y, e_ref) * (1.0 / RW_HEAD)
    yc = y - mean
    var = _head_sum(yc * yc, e_ref) * (1.0 / RW_HEAD)
    yn = yc * lax.rsqrt(var + GN_EPS) * lg_ref[...] + lb_ref[...]
    ya = yn * sz_ref[...] + bz_ref[...]
    ga = jax.nn.sigmoid(g_ref[:, 0:D_MODEL])
    gb = jax.nn.sigmoid(g_ref[:, D_MODEL:2 * D_MODEL])
    m = (ga * jnp.dot(ya.astype(BF16), wa_ref[...], preferred_element_type=F32)
         + gb * jnp.dot(yb_ref[...].astype(BF16), wb_ref[...], preferred_element_type=F32))
    o_ref[...] = x_ref[...] + jnp.dot(m.astype(BF16), wo_ref[...], preferred_element_type=F32)


def _out_proj(x2d, y, sz, bz, yb, pg, lnx_g, lnx_b, e, wa, wb, wo, tm):
    n = x2d.shape[0]
    rows = lambda c: pl.BlockSpec((tm, c), lambda i: (i, 0))
    full = lambda r, c: pl.BlockSpec((r, c), lambda i: (0, 0))
    return pl.pallas_call(
        _out_proj_body,
        out_shape=jax.ShapeDtypeStruct((n, D_MODEL), F32),
        grid=(n // tm,),
        in_specs=[rows(D_MODEL), rows(RW_WIDTH), rows(RW_WIDTH), rows(RW_WIDTH), rows(DA_WIDTH),
                  rows(G_COLS), full(1, RW_WIDTH), full(1, RW_WIDTH), full(RW_WIDTH, RW_WIDTH),
                  full(RW_WIDTH, D_MODEL), full(DA_WIDTH, D_MODEL), full(D_MODEL, D_MODEL)],
        out_specs=rows(D_MODEL),
        compiler_params=_cparams(("arbitrary",)),
        name="out_proj",
    )(x2d, y, sz, bz, yb, pg, lnx_g.reshape(1, -1), lnx_b.reshape(1, -1), e, wa, wb, wo)


def _trunk(x, t_valid, tm, tt, shift0, state0, cache_k, cache_v, page_table, prm):
    b, t, _ = x.shape
    n = b * t
    e = (jnp.arange(RW_WIDTH)[:, None] // RW_HEAD == jnp.arange(RW_WIDTH)[None, :] // RW_HEAD)
    e = e.astype(BF16)
    new_k, new_v, new_s, new_shift = [], [], [], []
    for l in range(2):
        w_in = prm["w_in"][l].astype(BF16)
        w_rw = w_in[:, :RW_COLS]
        x2d = x.reshape(n, D_MODEL)
        p_rw = _in_proj(x2d, prm["norm_g"][l], w_rw, tm)
        pda = _in_proj(x2d, prm["norm_g"][l], w_in[:, RW_COLS:RW_COLS + DA_COLS], tm)
        pg = _in_proj(x2d, prm["norm_g"][l], w_in[:, RW_COLS + DA_COLS:], tm)
        new_shift.append(_norm_rows(x[:, t_valid - 1], prm["norm_g"][l]))

        if shift0 is None:
            prev = jnp.zeros((b, 1, RW_COLS), F32)
            s_init = jnp.zeros((b, RW_HEADS, RW_HEAD, RW_HEAD), F32)
        else:
            prev = _in_proj(shift0[l], prm["norm_g"][l], w_rw, shift0[l].shape[0],
                            normalize=False).reshape(b, 1, RW_COLS)
            s_init = state0[l]
        r, w, k, v, rem, rep, sz, bz = _rwkv_prep(
            p_rw.reshape(b, t, RW_COLS), prev, prm["mu_shift"][l], prm["w0"][l], prm["w2"][l],
            prm["a0"][l], prm["a2"][l], prm["k_k"][l], prm["k_a"][l], prm["r_k"][l], e, tt)
        y, s_fin = _rwkv_scan(r, w, k, v, rem, rep, s_init, tt, t_valid)

        qn, kn = _da_prep(pda, prm["q_norm_g"][l], prm["k_norm_g"][l], e, tm)
        lam = jnp.stack([prm["lam_q1"][l], prm["lam_k1"][l], prm["lam_q2"][l], prm["lam_k2"][l]])
        pda3 = pda.reshape(b, t, DA_COLS)
        qn3 = qn.reshape(b, t, DA_WIDTH)
        kn3 = kn.reshape(b, t, DA_WIDTH)
        if cache_k is None:
            yb = _attn_prompt(qn3, kn3, pda3, prm["rel_bias"], lam, prm["subln_g"][l], l, tt)
        else:
            yb = _attn_sample(qn3, kn3, pda3, cache_k, cache_v, page_table, prm["rel_bias"], lam,
                              prm["subln_g"][l], l)

        x = _out_proj(x2d, y.reshape(n, RW_WIDTH), sz.reshape(n, RW_WIDTH),
                      bz.reshape(n, RW_WIDTH), yb.reshape(n, DA_WIDTH), pg, prm["lnx_g"][l],
                      prm["lnx_b"][l], e, prm["w_a_out"][l].astype(BF16),
                      prm["w_b_out"][l].astype(BF16), prm["w_o"][l].astype(BF16),
                      tm).reshape(b, t, D_MODEL)
        new_k.append(kn3[:, :t_valid].reshape(b, t_valid, DA_HEADS, DA_V))
        new_v.append(pda3[:, :t_valid, 2 * DA_WIDTH:3 * DA_WIDTH].reshape(b, t_valid, DA_HEADS, DA_V))
        new_s.append(s_fin)
    return x, jnp.stack(new_k), jnp.stack(new_v), jnp.stack(new_s), jnp.stack(new_shift)


def kernel(x_prompt, x_sample, cache_k, cache_v, page_table, state_rwkv, state_shift, meta_tokens,
           rel_bias, norm_g, w_in, mu_shift, w0, w2, a0, a2, k_k, k_a, r_k, lnx_g, lnx_b,
           q_norm_g, k_norm_g, lam_q1, lam_k1, lam_q2, lam_k2, subln_g, w_a_out, w_b_out, w_o):
    prm = dict(rel_bias=rel_bias, norm_g=norm_g, w_in=w_in, mu_shift=mu_shift, w0=w0, w2=w2,
               a0=a0, a2=a2, k_k=k_k, k_a=k_a, r_k=r_k.reshape(2, RW_WIDTH), lnx_g=lnx_g,
               lnx_b=lnx_b, q_norm_g=q_norm_g, k_norm_g=k_norm_g, lam_q1=lam_q1, lam_k1=lam_k1,
               lam_q2=lam_q2, lam_k2=lam_k2, subln_g=subln_g, w_a_out=w_a_out, w_b_out=w_b_out,
               w_o=w_o)
    bp, seq, _ = x_prompt.shape
    t_valid = seq + N_META
    t_pad = -(-t_valid // SEQ_TILE) * SEQ_TILE
    meta = jnp.broadcast_to(meta_tokens[None], (bp, N_META, D_MODEL))
    xp = jnp.concatenate([meta, x_prompt, jnp.zeros((bp, t_pad - t_valid, D_MODEL), F32)], axis=1)
    yp, kp, vp, sp, hp = _trunk(xp, t_valid, 512, SEQ_TILE, None, None, None, None, None, prm)

    ts = x_sample.shape[1]
    ys, k_s, v_s, s_s, h_s = _trunk(x_sample, ts, 512, ts, state_shift, state_rwkv, cache_k,
                                    cache_v, page_table, prm)
    return (yp[:, N_META:t_valid], ys, kp, vp, sp, hp, k_s, v_s, s_s, h_s)
```

```python
import functools
import math

import jax
import jax.numpy as jnp
from jax import lax
from jax.experimental import pallas as pl
from jax.experimental.pallas import tpu as pltpu

F32 = jnp.float32
BF16 = jnp.bfloat16

D_MODEL = 1024
N_META = 16
PAGE_SIZE = 128
RW_WIDTH = 512
RW_HEAD = 64
RW_HEADS = 8
LORA = 64
GN_EPS = 64e-5
DA_WIDTH = 512
DA_QK = 64
DA_HEADS = 4
DA_V = 128
N_BUCKETS = 32
MAX_DISTANCE = 128
NORM_EPS = 1e-6
NEG_INF = -1e30

RW_COLS = 3 * RW_WIDTH + 2 * LORA + RW_WIDTH
DA_COLS = 4 * DA_WIDTH
G_COLS = 2 * D_MODEL

SEQ_TILE = 128
VMEM_LIMIT = 48 * 1024 * 1024


def _cparams(sem):
    return pltpu.CompilerParams(dimension_semantics=sem, vmem_limit_bytes=VMEM_LIMIT)


def _head_sum(x, e_ref):
    hi = x.astype(BF16)
    lo = (x - hi.astype(F32)).astype(BF16)
    e = e_ref[...]
    return (jnp.dot(hi, e, preferred_element_type=F32)
            + jnp.dot(lo, e, preferred_element_type=F32))


def _in_proj_body(x_ref, g_ref, w_ref, o_ref, *, normalize):
    x = x_ref[...]
    if normalize:
        ms = jnp.mean(x * x, axis=-1, keepdims=True)
        x = x * lax.rsqrt(ms + NORM_EPS) * g_ref[...]
    o_ref[...] = jnp.dot(x.astype(BF16), w_ref[...], preferred_element_type=F32)


def _in_proj(x2d, g, w, tm, normalize=True):
    n, d = x2d.shape
    c = w.shape[1]
    return pl.pallas_call(
        functools.partial(_in_proj_body, normalize=normalize),
        out_shape=jax.ShapeDtypeStruct((n, c), F32),
        grid=(n // tm,),
        in_specs=[pl.BlockSpec((tm, d), lambda i: (i, 0)),
                  pl.BlockSpec((1, d), lambda i: (0, 0)),
                  pl.BlockSpec((d, c), lambda i: (0, 0))],
        out_specs=pl.BlockSpec((tm, c), lambda i: (i, 0)),
        compiler_params=_cparams(("arbitrary",)),
        name="in_proj",
    )(x2d, g.reshape(1, d), w)


def _norm_rows_body(x_ref, g_ref, o_ref):
    x = x_ref[...]
    ms = jnp.mean(x * x, axis=-1, keepdims=True)
    o_ref[...] = x * lax.rsqrt(ms + NORM_EPS) * g_ref[...]


def _norm_rows(x2d, g):
    n, d = x2d.shape
    return pl.pallas_call(
        _norm_rows_body,
        out_shape=jax.ShapeDtypeStruct((n, d), F32),
        name="norm_rows",
    )(x2d, g.reshape(1, d))


def _softplus(x):
    return jnp.maximum(x, 0.0) + jnp.log1p(jnp.exp(-jnp.abs(x)))


def _rwkv_prep_body(p_ref, prev_ref, mu_ref, w0_ref, w2_ref, a0_ref, a2_ref, kk_ref, ka_ref,
                    rk_ref, e_ref, r_ref, w_ref, k_ref, v_ref, rem_ref, rep_ref, sz_ref, bz_ref,
                    carry_ref, *, tt):
    @pl.when(pl.program_id(1) == 0)
    def _():
        carry_ref[...] = prev_ref[...]

    p = p_ref[...]
    rolled = pltpu.roll(p, 1, axis=0)
    row = lax.broadcasted_iota(jnp.int32, p.shape, 0)
    shifted = jnp.where(row == 0, carry_ref[...], rolled)
    carry_ref[...] = p[tt - 1:tt, :]
    h = p + (shifted - p) * mu_ref[...]

    r = h[:, 0:512]
    k = h[:, 512:1024]
    v = h[:, 1024:1536]
    wd = h[:, 1536:1600]
    ad = h[:, 1600:1664]
    z = h[:, 1664:2176]

    lw = w0_ref[...] + jnp.dot(jnp.tanh(wd).astype(BF16), w2_ref[...], preferred_element_type=F32)
    log_decay = -jnp.exp(-_softplus(-lw) - 0.5)
    a = jax.nn.sigmoid(a0_ref[...] + jnp.dot(ad.astype(BF16), a2_ref[...],
                                              preferred_element_type=F32))
    kk = k * kk_ref[...]
    ss = _head_sum(kk * kk, e_ref)
    kk = kk / jnp.maximum(jnp.sqrt(ss), 1e-12)
    kmod = k * (1.0 + (a - 1.0) * ka_ref[...])
    bonus = _head_sum(r * kmod * rk_ref[...], e_ref) * v
    sz = z * jax.nn.sigmoid(z)

    r_ref[...] = r
    w_ref[...] = log_decay
    k_ref[...] = kmod
    v_ref[...] = v
    rem_ref[...] = -kk
    rep_ref[...] = kk * a
    sz_ref[...] = sz
    bz_ref[...] = bonus * sz


def _rwkv_prep(p3, prev, mu, w0, w2, a0, a2, k_k, k_a, r_k, e, tt):
    b, t, _ = p3.shape
    row = lambda a: a.reshape(1, -1)
    seq_spec = pl.BlockSpec((None, tt, RW_WIDTH), lambda i, j: (i, j, 0))
    full = lambda shape: pl.BlockSpec(shape, lambda i, j: (0,) * len(shape))
    out = jax.ShapeDtypeStruct((b, t, RW_WIDTH), F32)
    return pl.pallas_call(
        functools.partial(_rwkv_prep_body, tt=tt),
        out_shape=(out,) * 8,
        grid=(b, t // tt),
        in_specs=[pl.BlockSpec((None, tt, RW_COLS), lambda i, j: (i, j, 0)),
                  pl.BlockSpec((None, 1, RW_COLS), lambda i, j: (i, 0, 0)),
                  full((1, RW_COLS)), full((1, RW_WIDTH)), full((LORA, RW_WIDTH)),
                  full((1, RW_WIDTH)), full((LORA, RW_WIDTH)), full((1, RW_WIDTH)),
                  full((1, RW_WIDTH)), full((1, RW_WIDTH)), full((RW_WIDTH, RW_WIDTH))],
        out_specs=(seq_spec,) * 8,
        scratch_shapes=[pltpu.VMEM((1, RW_COLS), F32)],
        compiler_params=_cparams(("arbitrary", "arbitrary")),
        name="rwkv_prep",
    )(p3, prev, row(mu), row(w0), w2.astype(BF16), row(a0), a2.astype(BF16), row(k_k), row(k_a),
      row(r_k), e)


def _rwkv_scan_body(r_ref, w_ref, k_ref, v_ref, rem_ref, rep_ref, s0_ref, y_ref, st_ref, s_ref,
                    *, tt, t_valid):
    tb = pl.program_id(1)

    @pl.when(tb == 0)
    def _():
        s_ref[...] = s0_ref[...]

    n = jnp.minimum(tt, t_valid - tb * tt)

    @pl.when(n < tt)
    def _():
        y_ref[...] = jnp.zeros_like(y_ref)

    eye = (lax.broadcasted_iota(jnp.int32, (RW_HEAD, RW_HEAD), 0)
           == lax.broadcasted_iota(jnp.int32, (RW_HEAD, RW_HEAD), 1))

    def step(t, carry):
        rr = r_ref[pl.ds(t, 1), :]
        ww = jnp.exp(w_ref[pl.ds(t, 1), :])
        kr = k_ref[pl.ds(t, 1), :]
        vr = v_ref[pl.ds(t, 1), :]
        ar = rem_ref[pl.ds(t, 1), :]
        br = rep_ref[pl.ds(t, 1), :]
        ys = []
        for h in range(RW_HEADS):
            sl = slice(h * RW_HEAD, (h + 1) * RW_HEAD)
            s = s_ref[h]
            v_col = jnp.sum(jnp.where(eye, vr[:, sl], 0.0), axis=1, keepdims=True)
            sa = jnp.sum(s * ar[:, sl], axis=1, keepdims=True)
            s = s * ww[:, sl] + sa * br[:, sl] + v_col * kr[:, sl]
            s_ref[h] = s
            y_col = jnp.sum(s * rr[:, sl], axis=1, keepdims=True)
            ys.append(jnp.sum(jnp.where(eye, y_col, 0.0), axis=0, keepdims=True))
        y_ref[pl.ds(t, 1), :] = jnp.concatenate(ys, axis=1)
        return carry

    lax.fori_loop(0, n, step, 0)

    @pl.when(tb == pl.num_programs(1) - 1)
    def _():
        st_ref[...] = s_ref[...]


def _rwkv_scan(r, w, k, v, rem, rep, s0, tt, t_valid):
    b, t, _ = r.shape
    seq_spec = pl.BlockSpec((None, tt, RW_WIDTH), lambda i, j: (i, j, 0))
    st_spec = pl.BlockSpec((None, RW_HEADS, RW_HEAD, RW_HEAD), lambda i, j: (i, 0, 0, 0))
    return pl.pallas_call(
        functools.partial(_rwkv_scan_body, tt=tt, t_valid=t_valid),
        out_shape=(jax.ShapeDtypeStruct((b, t, RW_WIDTH), F32),
                   jax.ShapeDtypeStruct((b, RW_HEADS, RW_HEAD, RW_HEAD), F32)),
        grid=(b, t // tt),
        in_specs=[seq_spec] * 6 + [st_spec],
        out_specs=(seq_spec, st_spec),
        scratch_shapes=[pltpu.VMEM((RW_HEADS, RW_HEAD, RW_HEAD), F32)],
        compiler_params=_cparams(("arbitrary", "arbitrary")),
        name="rwkv_scan",
    )(r, w, k, v, rem, rep, s0)


PAIR = 2 * RW_HEAD
N_PAIRS = RW_HEADS // 2
SCAN_CHUNK = 64
INV_BASE = 8


def _mm(a, b):
    return jnp.dot(a.astype(BF16), b.astype(BF16), preferred_element_type=F32)


def _cumsum_rows(x, n):
    row = lax.broadcasted_iota(jnp.int32, x.shape, 0)
    s = 1
    while s < n:
        x = x + jnp.where(row >= s, pltpu.roll(x, s, axis=0), 0.0)
        s *= 2
    return x


def _tri_inverse(m, n, blk_top):
    row = lax.broadcasted_iota(jnp.int32, (n, n), 0)
    col = lax.broadcasted_iota(jnp.int32, (n, n), 1)
    same = lambda c: (row // c) == (col // c)
    md = jnp.where(same(INV_BASE), m, 0.0)
    m2 = _mm(md, md)
    m4 = _mm(m2, m2)
    t = (row == col).astype(F32) + md
    t = t + _mm(t, m2)
    t = t + _mm(t, m4)
    c = INV_BASE
    while c < blk_top:
        off = jnp.where(same(2 * c) & jnp.logical_not(same(c)), m, 0.0)
        t = t + _mm(_mm(t, off), t)
        c *= 2
    return t


def _rwkv_chunk_scan_body(r_ref, lw_ref, k_ref, v_ref, a_ref, b_ref, s0_ref, y_ref, st_ref, h_ref,
                          *, chunk, t_valid):
    c = pl.program_id(1)
    n2 = 2 * chunk
    lane = lax.broadcasted_iota(jnp.int32, (chunk, PAIR), 1)
    lo_half = lane < RW_HEAD

    @pl.when(c == 0)
    def _():
        z = jnp.zeros((RW_HEAD, RW_HEAD), F32)
        for p in range(N_PAIRS):
            blk = jnp.concatenate([jnp.concatenate([s0_ref[2 * p], z], axis=1),
                                   jnp.concatenate([z, s0_ref[2 * p + 1]], axis=1)], axis=0)
            h_ref[p] = blk.T

    pos = c * chunk + lax.broadcasted_iota(jnp.int32, (chunk, RW_WIDTH), 0)
    valid = pos < t_valid
    msk = lambda ref: jnp.where(valid, ref[...], 0.0)
    r, lw, k, v, a, b = (msk(x) for x in (r_ref, lw_ref, k_ref, v_ref, a_ref, b_ref))

    g = _cumsum_rows(lw, chunk)
    gl = g[chunk - 1:chunk, :]
    e_neg = jnp.exp(-g)
    e_end = jnp.exp(gl - g)
    at = a * jnp.exp(g - lw)
    rt = r * jnp.exp(g)
    bt = b * e_neg
    kt = k * e_neg
    bh = b * e_end
    kh = k * e_end
    g_end = jnp.exp(gl)

    row2 = lax.broadcasted_iota(jnp.int32, (n2, n2), 0)
    col2 = lax.broadcasted_iota(jnp.int32, (n2, n2), 1)
    strict = row2 > col2
    incl = row2 >= col2
    zeros = jnp.zeros((n2, PAIR), F32)

    for p in range(N_PAIRS):
        sl = slice(p * PAIR, (p + 1) * PAIR)
        stack = lambda x: jnp.concatenate([jnp.where(lo_half, x[:, sl], 0.0),
                                           jnp.where(lo_half, 0.0, x[:, sl])], axis=0)
        a_st, r_st, b_st, k_st, bh_st, kh_st, v_st = (stack(x) for x in (at, rt, bt, kt, bh, kh, v))
        gram = lax.dot_general(jnp.concatenate([a_st, r_st], axis=0).astype(BF16),
                               jnp.concatenate([b_st, k_st], axis=0).astype(BF16),
                               _QK_DIMS, preferred_element_type=F32)
        mab = jnp.where(strict, gram[:n2, :n2], 0.0)
        mak = jnp.where(strict, gram[:n2, n2:], 0.0)
        mrb = jnp.where(incl, gram[n2:, :n2], 0.0)
        mrk = jnp.where(incl, gram[n2:, n2:], 0.0)
        t_inv = _tri_inverse(mab, n2, chunk)
        tx = _mm(t_inv, jnp.concatenate([_mm(mak, v_st), a_st], axis=1))
        u2 = tx[:, :PAIR]
        ah = tx[:, PAIR:]
        rh = jnp.concatenate([jnp.concatenate([ah, u2], axis=1),
                              jnp.concatenate([zeros, v_st], axis=1)], axis=0)
        qy = _mm(jnp.concatenate([mrb, mrk], axis=1), rh)
        gc = _mm(jnp.concatenate([bh_st, kh_st], axis=0).T, rh)
        h = h_ref[p]
        yst = _mm(r_st + qy[:, :PAIR], h) + qy[:, PAIR:]
        y_ref[:, sl] = yst[:chunk] + yst[chunk:]
        g_col = jnp.broadcast_to(g_end[:, sl], (PAIR, PAIR)).T
        h_ref[p] = g_col * h + _mm(gc[:, :PAIR], h) + gc[:, PAIR:]

    @pl.when(c == pl.num_programs(1) - 1)
    def _():
        for p in range(N_PAIRS):
            ht = h_ref[p].T
            st_ref[2 * p] = ht[:RW_HEAD, :RW_HEAD]
            st_ref[2 * p + 1] = ht[RW_HEAD:, RW_HEAD:]


def _rwkv_chunk_scan(r, lw, k, v, a, b, s0, t_valid):
    bsz, t, _ = r.shape
    seq = pl.BlockSpec((None, SCAN_CHUNK, RW_WIDTH), lambda i, j: (i, j, 0))
    st = pl.BlockSpec((None, RW_HEADS, RW_HEAD, RW_HEAD), lambda i, j: (i, 0, 0, 0))
    return pl.pallas_call(
        functools.partial(_rwkv_chunk_scan_body, chunk=SCAN_CHUNK, t_valid=t_valid),
        out_shape=(jax.ShapeDtypeStruct((bsz, t, RW_WIDTH), F32),
                   jax.ShapeDtypeStruct((bsz, RW_HEADS, RW_HEAD, RW_HEAD), F32)),
        grid=(bsz, t // SCAN_CHUNK),
        in_specs=[seq] * 6 + [st],
        out_specs=(seq, st),
        scratch_shapes=[pltpu.VMEM((N_PAIRS, PAIR, PAIR), F32)],
        compiler_params=_cparams(("arbitrary", "arbitrary")),
        name="rwkv_chunk_scan",
    )(r, lw, k, v, a, b, s0)


def _da_prep_body(p_ref, qg_ref, kg_ref, e_ref, q_ref, k_ref):
    q = p_ref[:, 0:512]
    k = p_ref[:, 512:1024]
    qm = _head_sum(q * q, e_ref) * (1.0 / DA_QK)
    km = _head_sum(k * k, e_ref) * (1.0 / DA_QK)
    q_ref[...] = (q * lax.rsqrt(qm + NORM_EPS) * qg_ref[...]).astype(BF16)
    k_ref[...] = k * lax.rsqrt(km + NORM_EPS) * kg_ref[...]


def _da_prep(pda, qg, kg, e, tm):
    n = pda.shape[0]
    tile8 = lambda g: jnp.tile(g, DA_WIDTH // DA_QK).reshape(1, DA_WIDTH)
    return pl.pallas_call(
        _da_prep_body,
        out_shape=(jax.ShapeDtypeStruct((n, DA_WIDTH), BF16),
                   jax.ShapeDtypeStruct((n, DA_WIDTH), F32)),
        grid=(n // tm,),
        in_specs=[pl.BlockSpec((tm, DA_COLS), lambda i: (i, 0)),
                  pl.BlockSpec((1, DA_WIDTH), lambda i: (0, 0)),
                  pl.BlockSpec((1, DA_WIDTH), lambda i: (0, 0)),
                  pl.BlockSpec((RW_WIDTH, RW_WIDTH), lambda i: (0, 0))],
        out_specs=(pl.BlockSpec((tm, DA_WIDTH), lambda i: (i, 0)),
                   pl.BlockSpec((tm, DA_WIDTH), lambda i: (i, 0))),
        compiler_params=_cparams(("arbitrary",)),
        name="da_prep",
    )(pda, tile8(qg), tile8(kg), e)


def _bias_tiles(n, rel_ref):
    max_exact = N_BUCKETS // 2
    nc = jnp.maximum(n, 0)
    nf = jnp.maximum(nc, 1).astype(F32)
    large = max_exact + (jnp.log(nf / max_exact) / math.log(MAX_DISTANCE / max_exact)
                         * (N_BUCKETS - max_exact)).astype(jnp.int32)
    large = jnp.minimum(large, N_BUCKETS - 1)
    bucket = jnp.where(nc < max_exact, nc, large)
    tiles = []
    for h in range(DA_HEADS):
        acc = jnp.zeros(n.shape, F32)
        for b in range(N_BUCKETS):
            acc = jnp.where(bucket == b, rel_ref[b, h], acc)
        tiles.append(jnp.where(n >= 0, acc, NEG_INF))
    return tiles


def _lambda(lam_ref, layer):
    lq1 = lam_ref[0:1, :]
    lk1 = lam_ref[1:2, :]
    lq2 = lam_ref[2:3, :]
    lk2 = lam_ref[3:4, :]
    li = 0.8 - 0.6 * math.exp(-0.3 * layer)
    lam = (jnp.exp(jnp.sum(lq1 * lk1, axis=1, keepdims=True))
           - jnp.exp(jnp.sum(lq2 * lk2, axis=1, keepdims=True)) + li)
    return lam, li


def _softmax_update(hm, s, vb, m_ref, l_ref, acc_ref):
    m_old = m_ref[hm]
    m_new = jnp.maximum(m_old, jnp.max(s, axis=1, keepdims=True))
    alpha = jnp.exp(m_old - m_new)
    p = jnp.exp(s - m_new)
    l_ref[hm] = alpha * l_ref[hm] + jnp.sum(p, axis=1, keepdims=True)
    acc_ref[hm] = alpha * acc_ref[hm] + jnp.dot(p.astype(BF16), vb, preferred_element_type=F32)
    m_ref[hm] = m_new


def _attn_finish(z, lam_ref, sg_ref, m_ref, l_ref, acc_ref, layer):
    lam, li = _lambda(lam_ref, layer)
    outs = []
    for h in range(DA_HEADS):
        o = acc_ref[2 * h] / l_ref[2 * h] - lam * (acc_ref[2 * h + 1] / l_ref[2 * h + 1])
        ms = jnp.mean(o * o, axis=1, keepdims=True)
        outs.append(o * lax.rsqrt(ms + NORM_EPS) * sg_ref[...] * (1.0 - li))
    return jnp.concatenate(outs, axis=1) * (z * jax.nn.sigmoid(z))


def _split_maps(qh):
    lane = lax.broadcasted_iota(jnp.int32, qh.shape, 1)
    zero = jnp.zeros_like(qh)
    return jnp.where(lane < DA_QK, qh, zero), jnp.where(lane >= DA_QK, qh, zero)


_QK_DIMS = (((1,), (1,)), ((), ()))


def _attn_prompt_body(q_ref, k_ref, v_ref, z_ref, rel_ref, lam_ref, sg_ref, o_ref,
                      m_ref, l_ref, acc_ref, bias_ref, *, tq, layer):
    b = pl.program_id(0)
    qi = pl.program_id(1)
    ki = pl.program_id(2)

    @pl.when((b == 0) & (qi == 0) & (ki == 0))
    def _():
        d = (lax.broadcasted_iota(jnp.int32, (tq, tq), 0)
             - lax.broadcasted_iota(jnp.int32, (tq, tq), 1))
        for blk in range(2):
            tiles = _bias_tiles(d + blk * tq, rel_ref)
            for h in range(DA_HEADS):
                bias_ref[blk * DA_HEADS + h] = tiles[h]

    @pl.when(ki == 0)
    def _():
        m_ref[...] = jnp.full(m_ref.shape, NEG_INF, F32)
        l_ref[...] = jnp.zeros_like(l_ref)
        acc_ref[...] = jnp.zeros_like(acc_ref)

    @pl.when(ki <= qi)
    def _():
        dblk = qi - ki
        for h in range(DA_HEADS):
            sl = slice(h * DA_V, (h + 1) * DA_V)
            kb = k_ref[:, sl].astype(BF16)
            vb = v_ref[:, sl].astype(BF16)
            far = jnp.full((tq, tq), rel_ref[N_BUCKETS - 1, h], F32)
            bias = jnp.where(dblk == 0, bias_ref[h],
                             jnp.where(dblk == 1, bias_ref[DA_HEADS + h], far))
            for mi, qm in enumerate(_split_maps(q_ref[:, sl])):
                s = lax.dot_general(qm, kb, _QK_DIMS, preferred_element_type=F32)
                s = s * (DA_QK ** -0.5) + bias
                _softmax_update(2 * h + mi, s, vb, m_ref, l_ref, acc_ref)

    @pl.when(ki == qi)
    def _():
        o_ref[...] = _attn_finish(z_ref[...], lam_ref, sg_ref, m_ref, l_ref, acc_ref, layer)


def _attn_prompt(qn, kn, pda, rel_bias, lam, sg, layer, tq):
    b, t, _ = qn.shape
    nq = t // tq
    return pl.pallas_call(
        functools.partial(_attn_prompt_body, tq=tq, layer=layer),
        out_shape=jax.ShapeDtypeStruct((b, t, DA_WIDTH), F32),
        grid=(b, nq, nq),
        in_specs=[pl.BlockSpec((None, tq, DA_WIDTH), lambda i, q, k: (i, q, 0)),
                  pl.BlockSpec((None, tq, DA_WIDTH), lambda i, q, k: (i, jnp.minimum(k, q), 0)),
                  pl.BlockSpec((None, tq, DA_WIDTH), lambda i, q, k: (i, jnp.minimum(k, q), 2)),
                  pl.BlockSpec((None, tq, DA_WIDTH), lambda i, q, k: (i, q, 3)),
                  pl.BlockSpec(memory_space=pltpu.SMEM),
                  pl.BlockSpec((4, DA_QK), lambda i, q, k: (0, 0)),
                  pl.BlockSpec((1, DA_V), lambda i, q, k: (0, 0))],
        out_specs=pl.BlockSpec((None, tq, DA_WIDTH), lambda i, q, k: (i, q, 0)),
        scratch_shapes=[pltpu.VMEM((2 * DA_HEADS, tq, 1), F32),
                        pltpu.VMEM((2 * DA_HEADS, tq, 1), F32),
                        pltpu.VMEM((2 * DA_HEADS, tq, DA_V), F32),
                        pltpu.VMEM((2 * DA_HEADS, tq, tq), F32)],
        compiler_params=_cparams(("arbitrary", "arbitrary", "arbitrary")),
        name="attn_prompt",
    )(qn, kn, pda, pda, rel_bias, lam, sg.reshape(1, DA_V))


def _attn_sample_body(pt_ref, q_ref, *refs, tq, npg, layer):
    kc = refs[:npg]
    vc = refs[npg:2 * npg]
    kn_ref, vn_ref, z_ref, rel_ref, lam_ref, sg_ref, o_ref, bias_ref = refs[2 * npg:]
    past = npg * PAGE_SIZE
    ncol = past + PAGE_SIZE

    @pl.when(pl.program_id(0) == 0)
    def _():
        t_idx = lax.rem(lax.broadcasted_iota(jnp.int32, (2 * tq, ncol), 0), tq)
        col = lax.broadcasted_iota(jnp.int32, (2 * tq, ncol), 1)
        tiles = _bias_tiles(past + t_idx - col, rel_ref)
        for h in range(DA_HEADS):
            bias_ref[h] = tiles[h]

    lam, li = _lambda(lam_ref, layer)
    pad = jnp.zeros((PAGE_SIZE - tq, DA_V), F32)
    outs = []
    for h in range(DA_HEADS):
        sl = slice(h * DA_V, (h + 1) * DA_V)
        q_st = jnp.concatenate(_split_maps(q_ref[:, sl]), axis=0)
        kh = jnp.concatenate([kc[j][:, h, :] for j in range(npg)] + [kn_ref[:, sl], pad], axis=0)
        vh = jnp.concatenate([vc[j][:, h, :] for j in range(npg)] + [vn_ref[:, sl], pad], axis=0)
        s = lax.dot_general(q_st, kh.astype(BF16), _QK_DIMS, preferred_element_type=F32)
        s = s * (DA_QK ** -0.5) + bias_ref[h]
        m = jnp.max(s, axis=1, keepdims=True)
        p = jnp.exp(s - m)
        l = jnp.sum(p, axis=1, keepdims=True)
        o = jnp.dot(p.astype(BF16), vh.astype(BF16), preferred_element_type=F32) / l
        o = o[:tq] - lam * o[tq:]
        ms = jnp.mean(o * o, axis=1, keepdims=True)
        outs.append(o * lax.rsqrt(ms + NORM_EPS) * sg_ref[...] * (1.0 - li))
    z = z_ref[...]
    o_ref[...] = jnp.concatenate(outs, axis=1) * (z * jax.nn.sigmoid(z))


def _attn_sample(qn, kn, pda, cache_k, cache_v, page_table, rel_bias, lam, sg, layer):
    b, tq, _ = qn.shape
    npg = page_table.shape[1]
    page_specs = [pl.BlockSpec((None, None, PAGE_SIZE, DA_HEADS, DA_V),
                               lambda i, pt, j=j: (layer, pt[i, j], 0, 0, 0)) for j in range(npg)]
    seq_spec = lambda col: pl.BlockSpec((None, tq, DA_WIDTH), lambda i, pt: (i, 0, col))
    grid_spec = pltpu.PrefetchScalarGridSpec(
        num_scalar_prefetch=1,
        grid=(b,),
        in_specs=[seq_spec(0)] + page_specs + page_specs
                 + [seq_spec(0), seq_spec(2), seq_spec(3),
                    pl.BlockSpec(memory_space=pltpu.SMEM),
                    pl.BlockSpec((4, DA_QK), lambda i, pt: (0, 0)),
                    pl.BlockSpec((1, DA_V), lambda i, pt: (0, 0))],
        out_specs=pl.BlockSpec((None, tq, DA_WIDTH), lambda i, pt: (i, 0, 0)),
        scratch_shapes=[pltpu.VMEM((DA_HEADS, 2 * tq, (npg + 1) * PAGE_SIZE), F32)],
    )
    return pl.pallas_call(
        functools.partial(_attn_sample_body, tq=tq, npg=npg, layer=layer),
        out_shape=jax.ShapeDtypeStruct((b, tq, DA_WIDTH), F32),
        grid_spec=grid_spec,
        compiler_params=_cparams(("arbitrary",)),
        name="attn_sample",
    )(page_table, qn, *([cache_k] * npg), *([cache_v] * npg), kn, pda, pda, rel_bias, lam,
      sg.reshape(1, DA_V))


def _out_proj_body(x_ref, y_ref, sz_ref, bz_ref, yb_ref, g_ref, lg_ref, lb_ref, e_ref,
                   wa_ref, wb_ref, wo_ref, o_ref):
    y = y_ref[...]
    mean = _head_sum(y, e_ref) * (1.0 / RW_HEAD)
    yc = y - mean
    var = _head_sum(yc * yc, e_ref) * (1.0 / RW_HEAD)
    yn = yc * lax.rsqrt(var + GN_EPS) * lg_ref[...] + lb_ref[...]
    ya = yn * sz_ref[...] + bz_ref[...]
    ga = jax.nn.sigmoid(g_ref[:, 0:D_MODEL])
    gb = jax.nn.sigmoid(g_ref[:, D_MODEL:2 * D_MODEL])
    m = (ga * jnp.dot(ya.astype(BF16), wa_ref[...], preferred_element_type=F32)
         + gb * jnp.dot(yb_ref[...].astype(BF16), wb_ref[...], preferred_element_type=F32))
    o_ref[...] = x_ref[...] + jnp.dot(m.astype(BF16), wo_ref[...], preferred_element_type=F32)


def _out_proj(x2d, y, sz, bz, yb, pg, lnx_g, lnx_b, e, wa, wb, wo, tm):
    n = x2d.shape[0]
    rows = lambda c: pl.BlockSpec((tm, c), lambda i: (i, 0))
    full = lambda r, c: pl.BlockSpec((r, c), lambda i: (0, 0))
    return pl.pallas_call(
        _out_proj_body,
        out_shape=jax.ShapeDtypeStruct((n, D_MODEL), F32),
        grid=(n // tm,),
        in_specs=[rows(D_MODEL), rows(RW_WIDTH), rows(RW_WIDTH), rows(RW_WIDTH), rows(DA_WIDTH),
                  rows(G_COLS), full(1, RW_WIDTH), full(1, RW_WIDTH), full(RW_WIDTH, RW_WIDTH),
                  full(RW_WIDTH, D_MODEL), full(DA_WIDTH, D_MODEL), full(D_MODEL, D_MODEL)],
        out_specs=rows(D_MODEL),
        compiler_params=_cparams(("arbitrary",)),
        name="out_proj",
    )(x2d, y, sz, bz, yb, pg, lnx_g.reshape(1, -1), lnx_b.reshape(1, -1), e, wa, wb, wo)


def _trunk(x, t_valid, tm, tt, shift0, state0, cache_k, cache_v, page_table, prm):
    b, t, _ = x.shape
    n = b * t
    e = (jnp.arange(RW_WIDTH)[:, None] // RW_HEAD == jnp.arange(RW_WIDTH)[None, :] // RW_HEAD)
    e = e.astype(BF16)
    new_k, new_v, new_s, new_shift = [], [], [], []
    for l in range(2):
        w_in = prm["w_in"][l].astype(BF16)
        w_rw = w_in[:, :RW_COLS]
        x2d = x.reshape(n, D_MODEL)
        p_rw = _in_proj(x2d, prm["norm_g"][l], w_rw, tm)
        pda = _in_proj(x2d, prm["norm_g"][l], w_in[:, RW_COLS:RW_COLS + DA_COLS], tm)
        pg = _in_proj(x2d, prm["norm_g"][l], w_in[:, RW_COLS + DA_COLS:], tm)
        new_shift.append(_norm_rows(x[:, t_valid - 1], prm["norm_g"][l]))

        if shift0 is None:
            prev = jnp.zeros((b, 1, RW_COLS), F32)
            s_init = jnp.zeros((b, RW_HEADS, RW_HEAD, RW_HEAD), F32)
        else:
            prev = _in_proj(shift0[l], prm["norm_g"][l], w_rw, shift0[l].shape[0],
                            normalize=False).reshape(b, 1, RW_COLS)
            s_init = state0[l]
        r, w, k, v, rem, rep, sz, bz = _rwkv_prep(
            p_rw.reshape(b, t, RW_COLS), prev, prm["mu_shift"][l], prm["w0"][l], prm["w2"][l],
            prm["a0"][l], prm["a2"][l], prm["k_k"][l], prm["k_a"][l], prm["r_k"][l], e, tt)
        if t % SCAN_CHUNK == 0:
            y, s_fin = _rwkv_chunk_scan(r, w, k, v, rem, rep, s_init, t_valid)
        else:
            y, s_fin = _rwkv_scan(r, w, k, v, rem, rep, s_init, tt, t_valid)

        qn, kn = _da_prep(pda, prm["q_norm_g"][l], prm["k_norm_g"][l], e, tm)
        lam = jnp.stack([prm["lam_q1"][l], prm["lam_k1"][l], prm["lam_q2"][l], prm["lam_k2"][l]])
        pda3 = pda.reshape(b, t, DA_COLS)
        qn3 = qn.reshape(b, t, DA_WIDTH)
        kn3 = kn.reshape(b, t, DA_WIDTH)
        if cache_k is None:
            yb = _attn_prompt(qn3, kn3, pda3, prm["rel_bias"], lam, prm["subln_g"][l], l, tt)
        else:
            yb = _attn_sample(qn3, kn3, pda3, cache_k, cache_v, page_table, prm["rel_bias"], lam,
                              prm["subln_g"][l], l)

        x = _out_proj(x2d, y.reshape(n, RW_WIDTH), sz.reshape(n, RW_WIDTH),
                      bz.reshape(n, RW_WIDTH), yb.reshape(n, DA_WIDTH), pg, prm["lnx_g"][l],
                      prm["lnx_b"][l], e, prm["w_a_out"][l].astype(BF16),
                      prm["w_b_out"][l].astype(BF16), prm["w_o"][l].astype(BF16),
                      tm).reshape(b, t, D_MODEL)
        new_k.append(kn3[:, :t_valid].reshape(b, t_valid, DA_HEADS, DA_V))
        new_v.append(pda3[:, :t_valid, 2 * DA_WIDTH:3 * DA_WIDTH].reshape(b, t_valid, DA_HEADS, DA_V))
        new_s.append(s_fin)
    return x, jnp.stack(new_k), jnp.stack(new_v), jnp.stack(new_s), jnp.stack(new_shift)


def kernel(x_prompt, x_sample, cache_k, cache_v, page_table, state_rwkv, state_shift, meta_tokens,
           rel_bias, norm_g, w_in, mu_shift, w0, w2, a0, a2, k_k, k_a, r_k, lnx_g, lnx_b,
           q_norm_g, k_norm_g, lam_q1, lam_k1, lam_q2, lam_k2, subln_g, w_a_out, w_b_out, w_o):
    prm = dict(rel_bias=rel_bias, norm_g=norm_g, w_in=w_in, mu_shift=mu_shift, w0=w0, w2=w2,
               a0=a0, a2=a2, k_k=k_k, k_a=k_a, r_k=r_k.reshape(2, RW_WIDTH), lnx_g=lnx_g,
               lnx_b=lnx_b, q_norm_g=q_norm_g, k_norm_g=k_norm_g, lam_q1=lam_q1, lam_k1=lam_k1,
               lam_q2=lam_q2, lam_k2=lam_k2, subln_g=subln_g, w_a_out=w_a_out, w_b_out=w_b_out,
               w_o=w_o)
    bp, seq, _ = x_prompt.shape
    t_valid = seq + N_META
    t_pad = -(-t_valid // SEQ_TILE) * SEQ_TILE
    meta = jnp.broadcast_to(meta_tokens[None], (bp, N_META, D_MODEL))
    xp = jnp.concatenate([meta, x_prompt, jnp.zeros((bp, t_pad - t_valid, D_MODEL), F32)], axis=1)
    yp, kp, vp, sp, hp = _trunk(xp, t_valid, 512, SEQ_TILE, None, None, None, None, None, prm)

    ts = x_sample.shape[1]
    ys, k_s, v_s, s_s, h_s = _trunk(x_sample, ts, 512, ts, state_shift, state_rwkv, cache_k,
                                    cache_v, page_table, prm)
    return (yp[:, N_META:t_valid], ys, kp, vp, sp, hp, k_s, v_s, s_s, h_s)
```

```python
import functools
import math

import jax
import jax.numpy as jnp
from jax import lax
from jax.experimental import pallas as pl
from jax.experimental.pallas import tpu as pltpu

F32 = jnp.float32
BF16 = jnp.bfloat16

D_MODEL = 1024
N_META = 16
PAGE_SIZE = 128
RW_WIDTH = 512
RW_HEAD = 64
RW_HEADS = 8
LORA = 64
GN_EPS = 64e-5
DA_WIDTH = 512
DA_QK = 64
DA_HEADS = 4
DA_V = 128
N_BUCKETS = 32
MAX_DISTANCE = 128
NORM_EPS = 1e-6
NEG_INF = -1e30

RW_COLS = 3 * RW_WIDTH + 2 * LORA + RW_WIDTH
DA_COLS = 4 * DA_WIDTH
G_COLS = 2 * D_MODEL

SEQ_TILE = 128
VMEM_LIMIT = 48 * 1024 * 1024


def _cparams(sem):
    return pltpu.CompilerParams(dimension_semantics=sem, vmem_limit_bytes=VMEM_LIMIT)


def _head_sum(x, e_ref):
    hi = x.astype(BF16)
    lo = (x - hi.astype(F32)).astype(BF16)
    e = e_ref[...]
    return (jnp.dot(hi, e, preferred_element_type=F32)
            + jnp.dot(lo, e, preferred_element_type=F32))


def _in_proj_body(x_ref, g_ref, w_ref, o_ref, *, normalize):
    x = x_ref[...]
    if normalize:
        ms = jnp.mean(x * x, axis=-1, keepdims=True)
        x = x * lax.rsqrt(ms + NORM_EPS) * g_ref[...]
    o_ref[...] = jnp.dot(x.astype(BF16), w_ref[...], preferred_element_type=F32)


def _in_proj(x2d, g, w, tm, normalize=True):
    n, d = x2d.shape
    c = w.shape[1]
    return pl.pallas_call(
        functools.partial(_in_proj_body, normalize=normalize),
        out_shape=jax.ShapeDtypeStruct((n, c), F32),
        grid=(n // tm,),
        in_specs=[pl.BlockSpec((tm, d), lambda i: (i, 0)),
                  pl.BlockSpec((1, d), lambda i: (0, 0)),
                  pl.BlockSpec((d, c), lambda i: (0, 0))],
        out_specs=pl.BlockSpec((tm, c), lambda i: (i, 0)),
        compiler_params=_cparams(("arbitrary",)),
        name="in_proj",
    )(x2d, g.reshape(1, d), w)


def _norm_rows_body(x_ref, g_ref, o_ref):
    x = x_ref[...]
    ms = jnp.mean(x * x, axis=-1, keepdims=True)
    o_ref[...] = x * lax.rsqrt(ms + NORM_EPS) * g_ref[...]


def _norm_rows(x2d, g):
    n, d = x2d.shape
    return pl.pallas_call(
        _norm_rows_body,
        out_shape=jax.ShapeDtypeStruct((n, d), F32),
        name="norm_rows",
    )(x2d, g.reshape(1, d))


def _softplus(x):
    return jnp.maximum(x, 0.0) + jnp.log1p(jnp.exp(-jnp.abs(x)))


def _rwkv_prep_body(p_ref, prev_ref, mu_ref, w0_ref, w2_ref, a0_ref, a2_ref, kk_ref, ka_ref,
                    rk_ref, e_ref, r_ref, w_ref, k_ref, v_ref, rem_ref, rep_ref, sz_ref, bz_ref,
                    carry_ref, *, tt):
    @pl.when(pl.program_id(1) == 0)
    def _():
        carry_ref[...] = prev_ref[...]

    p = p_ref[...]
    rolled = pltpu.roll(p, 1, axis=0)
    row = lax.broadcasted_iota(jnp.int32, p.shape, 0)
    shifted = jnp.where(row == 0, carry_ref[...], rolled)
    carry_ref[...] = p[tt - 1:tt, :]
    h = p + (shifted - p) * mu_ref[...]

    r = h[:, 0:512]
    k = h[:, 512:1024]
    v = h[:, 1024:1536]
    wd = h[:, 1536:1600]
    ad = h[:, 1600:1664]
    z = h[:, 1664:2176]

    lw = w0_ref[...] + jnp.dot(jnp.tanh(wd).astype(BF16), w2_ref[...], preferred_element_type=F32)
    log_decay = -jnp.exp(-_softplus(-lw) - 0.5)
    a = jax.nn.sigmoid(a0_ref[...] + jnp.dot(ad.astype(BF16), a2_ref[...],
                                              preferred_element_type=F32))
    kk = k * kk_ref[...]
    ss = _head_sum(kk * kk, e_ref)
    kk = kk / jnp.maximum(jnp.sqrt(ss), 1e-12)
    kmod = k * (1.0 + (a - 1.0) * ka_ref[...])
    bonus = _head_sum(r * kmod * rk_ref[...], e_ref) * v
    sz = z * jax.nn.sigmoid(z)

    r_ref[...] = r
    w_ref[...] = log_decay
    k_ref[...] = kmod
    v_ref[...] = v
    rem_ref[...] = -kk
    rep_ref[...] = kk * a
    sz_ref[...] = sz
    bz_ref[...] = bonus * sz


def _rwkv_prep(p3, prev, mu, w0, w2, a0, a2, k_k, k_a, r_k, e, tt):
    b, t, _ = p3.shape
    row = lambda a: a.reshape(1, -1)
    seq_spec = pl.BlockSpec((None, tt, RW_WIDTH), lambda i, j: (i, j, 0))
    full = lambda shape: pl.BlockSpec(shape, lambda i, j: (0,) * len(shape))
    out = jax.ShapeDtypeStruct((b, t, RW_WIDTH), F32)
    return pl.pallas_call(
        functools.partial(_rwkv_prep_body, tt=tt),
        out_shape=(out,) * 8,
        grid=(b, t // tt),
        in_specs=[pl.BlockSpec((None, tt, RW_COLS), lambda i, j: (i, j, 0)),
                  pl.BlockSpec((None, 1, RW_COLS), lambda i, j: (i, 0, 0)),
                  full((1, RW_COLS)), full((1, RW_WIDTH)), full((LORA, RW_WIDTH)),
                  full((1, RW_WIDTH)), full((LORA, RW_WIDTH)), full((1, RW_WIDTH)),
                  full((1, RW_WIDTH)), full((1, RW_WIDTH)), full((RW_WIDTH, RW_WIDTH))],
        out_specs=(seq_spec,) * 8,
        scratch_shapes=[pltpu.VMEM((1, RW_COLS), F32)],
        compiler_params=_cparams(("arbitrary", "arbitrary")),
        name="rwkv_prep",
    )(p3, prev, row(mu), row(w0), w2.astype(BF16), row(a0), a2.astype(BF16), row(k_k), row(k_a),
      row(r_k), e)


PAIR = 2 * RW_HEAD
N_PAIRS = RW_HEADS // 2
SCAN_CHUNK = 64
INV_BASE = 8


def _mm(a, b):
    return jnp.dot(a.astype(BF16), b.astype(BF16), preferred_element_type=F32)


def _seg_cumsum(x, seg, reverse=False):
    n = x.shape[0]
    t = lax.rem(lax.broadcasted_iota(jnp.int32, x.shape, 0), seg)
    y = x
    s = 1
    while s < seg:
        if reverse:
            y = y + jnp.where(t < seg - s, pltpu.roll(y, n - s, axis=0), 0.0)
        else:
            y = y + jnp.where(t >= s, pltpu.roll(y, s, axis=0), 0.0)
        s *= 2
    return y - x if reverse else y


def _rwkv_chunk_scan_body(r_ref, lw_ref, k_ref, v_ref, a_ref, b_ref, s0_ref, y_ref, st_ref, h_ref,
                          *, nseq, chunk, t_valid):
    c = pl.program_id(1)
    rows = nseq * chunk
    n2 = 2 * rows
    lane = lax.broadcasted_iota(jnp.int32, (rows, PAIR), 1)
    lo_half = lane < RW_HEAD

    @pl.when(c == 0)
    def _():
        z = jnp.zeros((RW_HEAD, RW_HEAD), F32)
        for s in range(nseq):
            for p in range(N_PAIRS):
                blk = jnp.concatenate([jnp.concatenate([s0_ref[s, 2 * p], z], axis=1),
                                       jnp.concatenate([z, s0_ref[s, 2 * p + 1]], axis=1)], axis=0)
                h_ref[s, p] = blk.T

    tpos = c * chunk + lax.rem(lax.broadcasted_iota(jnp.int32, (rows, RW_WIDTH), 0), chunk)
    valid = tpos < t_valid
    msk = lambda ref: jnp.where(valid, ref[...].reshape(rows, RW_WIDTH), 0.0)
    r, lw, k, v, a, b = (msk(x) for x in (r_ref, lw_ref, k_ref, v_ref, a_ref, b_ref))

    g = _seg_cumsum(lw, chunk)
    e_neg = jnp.exp(-g)
    e_end = jnp.exp(_seg_cumsum(lw, chunk, reverse=True))
    at = a * jnp.exp(g - lw)
    e_pos = jnp.exp(g)
    rt = r * e_pos
    bt = b * e_neg
    kt = k * e_neg
    bh = b * e_end
    kh = k * e_end

    row2 = lax.broadcasted_iota(jnp.int32, (n2, n2), 0)
    col2 = lax.broadcasted_iota(jnp.int32, (n2, n2), 1)
    same_seq = (row2 // chunk) == (col2 // chunk)
    strict = same_seq & (row2 > col2)
    incl = same_seq & (row2 >= col2)
    zeros = jnp.zeros((n2, PAIR), F32)
    seq_of_row = lax.rem(lax.broadcasted_iota(jnp.int32, (n2, PAIR), 0), rows) // chunk
    seq_of_col = lax.rem(lax.broadcasted_iota(jnp.int32, (PAIR, 2 * n2), 1), rows) // chunk

    pairs = range(N_PAIRS)
    sls = [slice(p * PAIR, (p + 1) * PAIR) for p in pairs]
    each = lambda f, *xs: [f(*(x[p] for x in xs)) for p in pairs]
    stack = lambda x: [jnp.concatenate([jnp.where(lo_half, x[:, sl], 0.0),
                                        jnp.where(lo_half, 0.0, x[:, sl])], axis=0) for sl in sls]
    a_st, r_st, b_st, k_st, bh_st, kh_st, v_st = (stack(x) for x in (at, rt, bt, kt, bh, kh, v))
    gram = each(lambda ar, bk: lax.dot_general(ar.astype(BF16), bk.astype(BF16), _QK_DIMS,
                                               preferred_element_type=F32),
                each(lambda x, y: jnp.concatenate([x, y], axis=0), a_st, r_st),
                each(lambda x, y: jnp.concatenate([x, y], axis=0), b_st, k_st))
    mab = each(lambda g_: jnp.where(strict, g_[:n2, :n2], 0.0), gram)
    mak = each(lambda g_: jnp.where(strict, g_[:n2, n2:], 0.0), gram)
    mrbk = each(lambda g_: jnp.concatenate([jnp.where(incl, g_[n2:, :n2], 0.0),
                                            jnp.where(incl, g_[n2:, n2:], 0.0)], axis=1), gram)
    x1 = each(_mm, mak, v_st)

    same = lambda w: (row2 // w) == (col2 // w)
    md = each(lambda m: jnp.where(same(INV_BASE), m, 0.0), mab)
    m2 = each(_mm, md, md)
    m4 = each(_mm, m2, m2)
    t_inv = each(lambda m: (row2 == col2).astype(F32) + m, md)
    t_inv = each(lambda t, m: t + _mm(t, m), t_inv, m2)
    t_inv = each(lambda t, m: t + _mm(t, m), t_inv, m4)
    width = INV_BASE
    while width < chunk:
        off = each(lambda m: jnp.where(same(2 * width) & jnp.logical_not(same(width)), m, 0.0), mab)
        t_off = each(_mm, t_inv, off)
        t_inv = each(lambda t, to: t + _mm(to, t), t_inv, t_off)
        width *= 2

    tx = each(lambda t, x, a_: _mm(t, jnp.concatenate([x, a_], axis=1)), t_inv, x1, a_st)
    rh = each(lambda t, v_: jnp.concatenate(
        [jnp.concatenate([t[:, PAIR:], t[:, :PAIR]], axis=1),
         jnp.concatenate([zeros, v_], axis=1)], axis=0), tx, v_st)
    qy = each(_mm, mrbk, rh)
    q = each(lambda r_, qy_: r_ + qy_[:, :PAIR], r_st, qy)
    yst = each(lambda qy_: qy_[:, PAIR:], qy)
    bk_t = each(lambda x, y: jnp.concatenate([x, y], axis=0).T, bh_st, kh_st)
    only = nseq > 1
    for s in range(nseq):
        h = [h_ref[s, p] for p in pairs]
        yst = each(lambda y_, q_, h_: y_ + _mm(jnp.where(seq_of_row == s, q_, 0.0) if only else q_,
                                               h_), yst, q, h)
        gc = each(lambda b_, r_: _mm(jnp.where(seq_of_col == s, b_, 0.0) if only else b_, r_),
                  bk_t, rh)
        gh = each(lambda g_, h_: _mm(g_[:, :PAIR], h_), gc, h)
        last = s * chunk + chunk - 1
        for p in pairs:
            g_col = jnp.broadcast_to(e_pos[last:last + 1, sls[p]], (PAIR, PAIR)).T
            h_ref[s, p] = g_col * h[p] + gh[p] + gc[p][:, PAIR:]
    y_ref[...] = jnp.concatenate([y_[:rows] + y_[rows:] for y_ in yst], axis=1).reshape(y_ref.shape)

    @pl.when(c == pl.num_programs(1) - 1)
    def _():
        for s in range(nseq):
            for p in range(N_PAIRS):
                ht = h_ref[s, p].T
                st_ref[s, 2 * p] = ht[:RW_HEAD, :RW_HEAD]
                st_ref[s, 2 * p + 1] = ht[RW_HEAD:, RW_HEAD:]


def _rwkv_chunk_scan(r, lw, k, v, a, b, s0, t_valid):
    bsz, t, _ = r.shape
    chunk = min(t, SCAN_CHUNK)
    nseq = SCAN_CHUNK // chunk
    seq = pl.BlockSpec((nseq, chunk, RW_WIDTH), lambda i, j: (i, j, 0))
    st = pl.BlockSpec((nseq, RW_HEADS, RW_HEAD, RW_HEAD), lambda i, j: (i, 0, 0, 0))
    return pl.pallas_call(
        functools.partial(_rwkv_chunk_scan_body, nseq=nseq, chunk=chunk, t_valid=t_valid),
        out_shape=(jax.ShapeDtypeStruct((bsz, t, RW_WIDTH), F32),
                   jax.ShapeDtypeStruct((bsz, RW_HEADS, RW_HEAD, RW_HEAD), F32)),
        grid=(bsz // nseq, t // chunk),
        in_specs=[seq] * 6 + [st],
        out_specs=(seq, st),
        scratch_shapes=[pltpu.VMEM((nseq, N_PAIRS, PAIR, PAIR), F32)],
        compiler_params=_cparams(("arbitrary", "arbitrary")),
        name="rwkv_chunk_scan",
    )(r, lw, k, v, a, b, s0)


def _da_prep_body(p_ref, qg_ref, kg_ref, e_ref, q_ref, k_ref, kb_ref, vb_ref):
    q = p_ref[:, 0:512]
    k = p_ref[:, 512:1024]
    qm = _head_sum(q * q, e_ref) * (1.0 / DA_QK)
    km = _head_sum(k * k, e_ref) * (1.0 / DA_QK)
    qn = q * lax.rsqrt(qm + NORM_EPS) * qg_ref[...]
    kn = k * lax.rsqrt(km + NORM_EPS) * kg_ref[...]
    q_ref[...] = (qn * (DA_QK ** -0.5)).astype(BF16)
    k_ref[...] = kn
    kb_ref[...] = kn.astype(BF16)
    vb_ref[...] = p_ref[:, 1024:1536].astype(BF16)


def _da_prep(pda, qg, kg, e, tm):
    n = pda.shape[0]
    tile8 = lambda g: jnp.tile(g, DA_WIDTH // DA_QK).reshape(1, DA_WIDTH)
    rows = pl.BlockSpec((tm, DA_WIDTH), lambda i: (i, 0))
    bf = jax.ShapeDtypeStruct((n, DA_WIDTH), BF16)
    return pl.pallas_call(
        _da_prep_body,
        out_shape=(bf, jax.ShapeDtypeStruct((n, DA_WIDTH), F32), bf, bf),
        grid=(n // tm,),
        in_specs=[pl.BlockSpec((tm, DA_COLS), lambda i: (i, 0)),
                  pl.BlockSpec((1, DA_WIDTH), lambda i: (0, 0)),
                  pl.BlockSpec((1, DA_WIDTH), lambda i: (0, 0)),
                  pl.BlockSpec((RW_WIDTH, RW_WIDTH), lambda i: (0, 0))],
        out_specs=(rows, rows, rows, rows),
        compiler_params=_cparams(("arbitrary",)),
        name="da_prep",
    )(pda, tile8(qg), tile8(kg), e)


def _bias_tiles(n, rel_ref):
    max_exact = N_BUCKETS // 2
    nc = jnp.maximum(n, 0)
    nf = jnp.maximum(nc, 1).astype(F32)
    large = max_exact + (jnp.log(nf / max_exact) / math.log(MAX_DISTANCE / max_exact)
                         * (N_BUCKETS - max_exact)).astype(jnp.int32)
    large = jnp.minimum(large, N_BUCKETS - 1)
    bucket = jnp.where(nc < max_exact, nc, large)
    tiles = []
    for h in range(DA_HEADS):
        acc = jnp.zeros(n.shape, F32)
        for b in range(N_BUCKETS):
            acc = jnp.where(bucket == b, rel_ref[b, h], acc)
        tiles.append(jnp.where(n >= 0, acc, NEG_INF))
    return tiles


def _lambda(lam_ref, layer):
    lq1 = lam_ref[0:1, :]
    lk1 = lam_ref[1:2, :]
    lq2 = lam_ref[2:3, :]
    lk2 = lam_ref[3:4, :]
    li = 0.8 - 0.6 * math.exp(-0.3 * layer)
    lam = (jnp.exp(jnp.sum(lq1 * lk1, axis=1, keepdims=True))
           - jnp.exp(jnp.sum(lq2 * lk2, axis=1, keepdims=True)) + li)
    return lam, li


def _softmax_update(carries, q_sts, kbs, vbs, biases):
    heads = range(len(q_sts))
    ss = [lax.dot_general(q_sts[h], kbs[h], _QK_DIMS, preferred_element_type=F32) + biases[h]
          for h in heads]
    ms, ls, ps, alphas = [], [], [], []
    for h in heads:
        m, l, _ = carries[h]
        m_new = jnp.maximum(m, jnp.max(ss[h], axis=1, keepdims=True))
        alpha = jnp.exp(m - m_new)
        p = jnp.exp(ss[h] - m_new)
        ms.append(m_new)
        ls.append(alpha * l + jnp.sum(p, axis=1, keepdims=True))
        ps.append(p.astype(BF16))
        alphas.append(alpha)
    pvs = [jnp.dot(ps[h], vbs[h], preferred_element_type=F32) for h in heads]
    return tuple((ms[h], ls[h], alphas[h] * carries[h][2] + pvs[h]) for h in heads)


def _head_out(o_st, z, lam, li, sg_ref, tq):
    o = o_st[:tq] - lam * o_st[tq:]
    ms = jnp.mean(o * o, axis=1, keepdims=True)
    return o * lax.rsqrt(ms + NORM_EPS) * sg_ref[...] * (1.0 - li) * (z * jax.nn.sigmoid(z))


def _split_maps(qh):
    lane = lax.broadcasted_iota(jnp.int32, qh.shape, 1)
    zero = jnp.zeros_like(qh)
    return jnp.where(lane < DA_QK, qh, zero), jnp.where(lane >= DA_QK, qh, zero)


_QK_DIMS = (((1,), (1,)), ((), ()))


def _attn_prompt_body(q_ref, k_ref, v_ref, z_ref, rel_ref, lam_ref, sg_ref, o_ref, bias_ref,
                      *, tq, layer):
    qi = pl.program_id(1)

    @pl.when((pl.program_id(0) == 0) & (qi == 0))
    def _():
        d = (lax.broadcasted_iota(jnp.int32, (tq, tq), 0)
             - lax.broadcasted_iota(jnp.int32, (tq, tq), 1))
        for blk in range(2):
            tiles = _bias_tiles(d + blk * tq, rel_ref)
            for h in range(DA_HEADS):
                bias_ref[blk * DA_HEADS + h] = tiles[h]

    lam, li = _lambda(lam_ref, layer)
    n_far = jnp.maximum(qi - 1, 0)
    prev_pen = jnp.where(qi >= 1, 0.0, NEG_INF)
    off_prev = pl.multiple_of(n_far * tq, tq)
    off_diag = pl.multiple_of(qi * tq, tq)
    heads = range(DA_HEADS)
    sls = [slice(h * DA_V, (h + 1) * DA_V) for h in heads]
    q_sts = [jnp.concatenate(_split_maps(q_ref[:, sl]), axis=0) for sl in sls]

    def attend(carries, off, biases):
        return _softmax_update(carries, q_sts, [k_ref[pl.ds(off, tq), sl] for sl in sls],
                               [v_ref[pl.ds(off, tq), sl] for sl in sls], biases)

    far_bias = [rel_ref[N_BUCKETS - 1, h] for h in heads]
    init = (jnp.full((2 * tq, 1), NEG_INF, F32), jnp.zeros((2 * tq, 1), F32),
            jnp.zeros((2 * tq, DA_V), F32))
    carries = lax.fori_loop(
        0, n_far, lambda ki, c: attend(c, pl.multiple_of(ki * tq, tq), far_bias), (init,) * DA_HEADS)
    twice = lambda x: jnp.concatenate([x, x], axis=0)
    carries = attend(carries, off_prev, [twice(bias_ref[DA_HEADS + h] + prev_pen) for h in heads])
    carries = attend(carries, off_diag, [twice(bias_ref[h]) for h in heads])
    for h in heads:
        _, l, acc = carries[h]
        o_ref[:, sls[h]] = _head_out(acc / l, z_ref[:, sls[h]], lam, li, sg_ref, tq)


def _attn_prompt(qn, kb, vb, pda, rel_bias, lam, sg, layer, tq):
    b, t, _ = qn.shape
    return pl.pallas_call(
        functools.partial(_attn_prompt_body, tq=tq, layer=layer),
        out_shape=jax.ShapeDtypeStruct((b, t, DA_WIDTH), F32),
        grid=(b, t // tq),
        in_specs=[pl.BlockSpec((None, tq, DA_WIDTH), lambda i, q: (i, q, 0)),
                  pl.BlockSpec((None, t, DA_WIDTH), lambda i, q: (i, 0, 0)),
                  pl.BlockSpec((None, t, DA_WIDTH), lambda i, q: (i, 0, 0)),
                  pl.BlockSpec((None, tq, DA_WIDTH), lambda i, q: (i, q, 3)),
                  pl.BlockSpec(memory_space=pltpu.SMEM),
                  pl.BlockSpec((4, DA_QK), lambda i, q: (0, 0)),
                  pl.BlockSpec((1, DA_V), lambda i, q: (0, 0))],
        out_specs=pl.BlockSpec((None, tq, DA_WIDTH), lambda i, q: (i, q, 0)),
        scratch_shapes=[pltpu.VMEM((2 * DA_HEADS, tq, tq), F32)],
        compiler_params=_cparams(("arbitrary", "arbitrary")),
        name="attn_prompt",
    )(qn, kb, vb, pda, rel_bias, lam, sg.reshape(1, DA_V))


def _attn_sample_body(pt_ref, q_ref, *refs, tq, npg, layer):
    kc = refs[:npg]
    vc = refs[npg:2 * npg]
    kn_ref, vn_ref, z_ref, rel_ref, lam_ref, sg_ref, o_ref, bias_ref = refs[2 * npg:]
    past = npg * PAGE_SIZE
    ncol = past + PAGE_SIZE

    @pl.when(pl.program_id(0) == 0)
    def _():
        t_idx = lax.rem(lax.broadcasted_iota(jnp.int32, (2 * tq, ncol), 0), tq)
        col = lax.broadcasted_iota(jnp.int32, (2 * tq, ncol), 1)
        tiles = _bias_tiles(past + t_idx - col, rel_ref)
        for h in range(DA_HEADS):
            bias_ref[h] = tiles[h]

    lam, li = _lambda(lam_ref, layer)
    pad = jnp.zeros((PAGE_SIZE - tq, DA_V), F32)
    outs = []
    for h in range(DA_HEADS):
        sl = slice(h * DA_V, (h + 1) * DA_V)
        q_st = jnp.concatenate(_split_maps(q_ref[:, sl]), axis=0)
        kh = jnp.concatenate([kc[j][:, h, :] for j in range(npg)] + [kn_ref[:, sl], pad], axis=0)
        vh = jnp.concatenate([vc[j][:, h, :] for j in range(npg)] + [vn_ref[:, sl], pad], axis=0)
        s = lax.dot_general(q_st, kh.astype(BF16), _QK_DIMS, preferred_element_type=F32)
        s = s + bias_ref[h]
        m = jnp.max(s, axis=1, keepdims=True)
        p = jnp.exp(s - m)
        l = jnp.sum(p, axis=1, keepdims=True)
        o = jnp.dot(p.astype(BF16), vh.astype(BF16), preferred_element_type=F32) / l
        outs.append(_head_out(o, z_ref[:, sl], lam, li, sg_ref, tq))
    o_ref[...] = jnp.concatenate(outs, axis=1)


def _attn_sample(qn, kn, pda, cache_k, cache_v, page_table, rel_bias, lam, sg, layer):
    b, tq, _ = qn.shape
    npg = page_table.shape[1]
    page_specs = [pl.BlockSpec((None, None, PAGE_SIZE, DA_HEADS, DA_V),
                               lambda i, pt, j=j: (layer, pt[i, j], 0, 0, 0)) for j in range(npg)]
    seq_spec = lambda col: pl.BlockSpec((None, tq, DA_WIDTH), lambda i, pt: (i, 0, col))
    grid_spec = pltpu.PrefetchScalarGridSpec(
        num_scalar_prefetch=1,
        grid=(b,),
        in_specs=[seq_spec(0)] + page_specs + page_specs
                 + [seq_spec(0), seq_spec(2), seq_spec(3),
                    pl.BlockSpec(memory_space=pltpu.SMEM),
                    pl.BlockSpec((4, DA_QK), lambda i, pt: (0, 0)),
                    pl.BlockSpec((1, DA_V), lambda i, pt: (0, 0))],
        out_specs=pl.BlockSpec((None, tq, DA_WIDTH), lambda i, pt: (i, 0, 0)),
        scratch_shapes=[pltpu.VMEM((DA_HEADS, 2 * tq, (npg + 1) * PAGE_SIZE), F32)],
    )
    return pl.pallas_call(
        functools.partial(_attn_sample_body, tq=tq, npg=npg, layer=layer),
        out_shape=jax.ShapeDtypeStruct((b, tq, DA_WIDTH), F32),
        grid_spec=grid_spec,
        compiler_params=_cparams(("arbitrary",)),
        name="attn_sample",
    )(page_table, qn, *([cache_k] * npg), *([cache_v] * npg), kn, pda, pda, rel_bias, lam,
      sg.reshape(1, DA_V))


def _out_proj_body(x_ref, y_ref, sz_ref, bz_ref, yb_ref, g_ref, lg_ref, lb_ref, e_ref,
                   wa_ref, wb_ref, wo_ref, o_ref):
    y = y_ref[...]
    mean = _head_sum(y, e_ref) * (1.0 / RW_HEAD)
    yc = y - mean
    var = _head_sum(yc * yc, e_ref) * (1.0 / RW_HEAD)
    yn = yc * lax.rsqrt(var + GN_EPS) * lg_ref[...] + lb_ref[...]
    ya = yn * sz_ref[...] + bz_ref[...]
    ga = jax.nn.sigmoid(g_ref[:, 0:D_MODEL])
    gb = jax.nn.sigmoid(g_ref[:, D_MODEL:2 * D_MODEL])
    m = (ga * jnp.dot(ya.astype(BF16), wa_ref[...], preferred_element_type=F32)
         + gb * jnp.dot(yb_ref[...].astype(BF16), wb_ref[...], preferred_element_type=F32))
    o_ref[...] = x_ref[...] + jnp.dot(m.astype(BF16), wo_ref[...], preferred_element_type=F32)


def _out_proj(x2d, y, sz, bz, yb, pg, lnx_g, lnx_b, e, wa, wb, wo, tm):
    n = x2d.shape[0]
    rows = lambda c: pl.BlockSpec((tm, c), lambda i: (i, 0))
    full = lambda r, c: pl.BlockSpec((r, c), lambda i: (0, 0))
    return pl.pallas_call(
        _out_proj_body,
        out_shape=jax.ShapeDtypeStruct((n, D_MODEL), F32),
        grid=(n // tm,),
        in_specs=[rows(D_MODEL), rows(RW_WIDTH), rows(RW_WIDTH), rows(RW_WIDTH), rows(DA_WIDTH),
                  rows(G_COLS), full(1, RW_WIDTH), full(1, RW_WIDTH), full(RW_WIDTH, RW_WIDTH),
                  full(RW_WIDTH, D_MODEL), full(DA_WIDTH, D_MODEL), full(D_MODEL, D_MODEL)],
        out_specs=rows(D_MODEL),
        compiler_params=_cparams(("arbitrary",)),
        name="out_proj",
    )(x2d, y, sz, bz, yb, pg, lnx_g.reshape(1, -1), lnx_b.reshape(1, -1), e, wa, wb, wo)


def _trunk(x, t_valid, tm, tt, shift0, state0, cache_k, cache_v, page_table, prm):
    b, t, _ = x.shape
    n = b * t
    e = (jnp.arange(RW_WIDTH)[:, None] // RW_HEAD == jnp.arange(RW_WIDTH)[None, :] // RW_HEAD)
    e = e.astype(BF16)
    new_k, new_v, new_s, new_shift = [], [], [], []
    for l in range(2):
        w_in = prm["w_in"][l].astype(BF16)
        w_rw = w_in[:, :RW_COLS]
        x2d = x.reshape(n, D_MODEL)
        p_rw = _in_proj(x2d, prm["norm_g"][l], w_rw, tm)
        pda = _in_proj(x2d, prm["norm_g"][l], w_in[:, RW_COLS:RW_COLS + DA_COLS], tm)
        pg = _in_proj(x2d, prm["norm_g"][l], w_in[:, RW_COLS + DA_COLS:], tm)
        new_shift.append(_norm_rows(x[:, t_valid - 1], prm["norm_g"][l]))

        if shift0 is None:
            prev = jnp.zeros((b, 1, RW_COLS), F32)
            s_init = jnp.zeros((b, RW_HEADS, RW_HEAD, RW_HEAD), F32)
        else:
            prev = _in_proj(shift0[l], prm["norm_g"][l], w_rw, shift0[l].shape[0],
                            normalize=False).reshape(b, 1, RW_COLS)
            s_init = state0[l]
        r, w, k, v, rem, rep, sz, bz = _rwkv_prep(
            p_rw.reshape(b, t, RW_COLS), prev, prm["mu_shift"][l], prm["w0"][l], prm["w2"][l],
            prm["a0"][l], prm["a2"][l], prm["k_k"][l], prm["k_a"][l], prm["r_k"][l], e, tt)
        y, s_fin = _rwkv_chunk_scan(r, w, k, v, rem, rep, s_init, t_valid)

        qn, kn, kb, vb = _da_prep(pda, prm["q_norm_g"][l], prm["k_norm_g"][l], e, tm)
        lam = jnp.stack([prm["lam_q1"][l], prm["lam_k1"][l], prm["lam_q2"][l], prm["lam_k2"][l]])
        pda3 = pda.reshape(b, t, DA_COLS)
        qn3 = qn.reshape(b, t, DA_WIDTH)
        kn3 = kn.reshape(b, t, DA_WIDTH)
        if cache_k is None:
            yb = _attn_prompt(qn3, kb.reshape(b, t, DA_WIDTH), vb.reshape(b, t, DA_WIDTH), pda3,
                              prm["rel_bias"], lam, prm["subln_g"][l], l, tt)
        else:
            yb = _attn_sample(qn3, kn3, pda3, cache_k, cache_v, page_table, prm["rel_bias"], lam,
                              prm["subln_g"][l], l)

        x = _out_proj(x2d, y.reshape(n, RW_WIDTH), sz.reshape(n, RW_WIDTH),
                      bz.reshape(n, RW_WIDTH), yb.reshape(n, DA_WIDTH), pg, prm["lnx_g"][l],
                      prm["lnx_b"][l], e, prm["w_a_out"][l].astype(BF16),
                      prm["w_b_out"][l].astype(BF16), prm["w_o"][l].astype(BF16),
                      tm).reshape(b, t, D_MODEL)
        new_k.append(kn3[:, :t_valid].reshape(b, t_valid, DA_HEADS, DA_V))
        new_v.append(pda3[:, :t_valid, 2 * DA_WIDTH:3 * DA_WIDTH].reshape(b, t_valid, DA_HEADS, DA_V))
        new_s.append(s_fin)
    return x, jnp.stack(new_k), jnp.stack(new_v), jnp.stack(new_s), jnp.stack(new_shift)


def kernel(x_prompt, x_sample, cache_k, cache_v, page_table, state_rwkv, state_shift, meta_tokens,
           rel_bias, norm_g, w_in, mu_shift, w0, w2, a0, a2, k_k, k_a, r_k, lnx_g, lnx_b,
           q_norm_g, k_norm_g, lam_q1, lam_k1, lam_q2, lam_k2, subln_g, w_a_out, w_b_out, w_o):
    prm = dict(rel_bias=rel_bias, norm_g=norm_g, w_in=w_in, mu_shift=mu_shift, w0=w0, w2=w2,
               a0=a0, a2=a2, k_k=k_k, k_a=k_a, r_k=r_k.reshape(2, RW_WIDTH), lnx_g=lnx_g,
               lnx_b=lnx_b, q_norm_g=q_norm_g, k_norm_g=k_norm_g, lam_q1=lam_q1, lam_k1=lam_k1,
               lam_q2=lam_q2, lam_k2=lam_k2, subln_g=subln_g, w_a_out=w_a_out, w_b_out=w_b_out,
               w_o=w_o)
    bp, seq, _ = x_prompt.shape
    t_valid = seq + N_META
    t_pad = -(-t_valid // SEQ_TILE) * SEQ_TILE
    meta = jnp.broadcast_to(meta_tokens[None], (bp, N_META, D_MODEL))
    xp = jnp.concatenate([meta, x_prompt, jnp.zeros((bp, t_pad - t_valid, D_MODEL), F32)], axis=1)
    yp, kp, vp, sp, hp = _trunk(xp, t_valid, 512, SEQ_TILE, None, None, None, None, None, prm)

    ts = x_sample.shape[1]
    ys, k_s, v_s, s_s, h_s = _trunk(x_sample, ts, 512, ts, state_shift, state_rwkv, cache_k,
                                    cache_v, page_table, prm)
    return (yp[:, N_META:t_valid], ys, kp, vp, sp, hp, k_s, v_s, s_s, h_s)
```

```python
import functools
import math

import jax
import jax.numpy as jnp
import numpy as np
from jax import lax
from jax.experimental import pallas as pl
from jax.experimental.pallas import tpu as pltpu

F32 = jnp.float32
BF16 = jnp.bfloat16

D_MODEL = 1024
N_META = 16
PAGE_SIZE = 128
RW_WIDTH = 512
RW_HEAD = 64
RW_HEADS = 8
LORA = 64
GN_EPS = 64e-5
DA_WIDTH = 512
DA_QK = 64
DA_HEADS = 4
DA_V = 128
N_BUCKETS = 32
MAX_DISTANCE = 128
NORM_EPS = 1e-6
NEG_INF = -1e30

RW_COLS = 3 * RW_WIDTH + 2 * LORA + RW_WIDTH
DA_COLS = 4 * DA_WIDTH
G_COLS = 2 * D_MODEL

SEQ_TILE = 128
VMEM_LIMIT = 48 * 1024 * 1024


def _cparams(sem):
    return pltpu.CompilerParams(dimension_semantics=sem, vmem_limit_bytes=VMEM_LIMIT)


def _head_sum(x, e_ref):
    hi = x.astype(BF16)
    lo = (x - hi.astype(F32)).astype(BF16)
    e = e_ref[...]
    return (jnp.dot(hi, e, preferred_element_type=F32)
            + jnp.dot(lo, e, preferred_element_type=F32))


def _in_proj_body(x_ref, g_ref, w_ref, o_ref, *, normalize):
    x = x_ref[...]
    if normalize:
        ms = jnp.mean(x * x, axis=-1, keepdims=True)
        x = x * lax.rsqrt(ms + NORM_EPS) * g_ref[...]
    o_ref[...] = jnp.dot(x.astype(BF16), w_ref[...], preferred_element_type=F32)


def _in_proj(x2d, g, w, tm, normalize=True):
    n, d = x2d.shape
    c = w.shape[1]
    return pl.pallas_call(
        functools.partial(_in_proj_body, normalize=normalize),
        out_shape=jax.ShapeDtypeStruct((n, c), F32),
        grid=(n // tm,),
        in_specs=[pl.BlockSpec((tm, d), lambda i: (i, 0)),
                  pl.BlockSpec((1, d), lambda i: (0, 0)),
                  pl.BlockSpec((d, c), lambda i: (0, 0))],
        out_specs=pl.BlockSpec((tm, c), lambda i: (i, 0)),
        compiler_params=_cparams(("arbitrary",)),
        name="in_proj",
    )(x2d, g.reshape(1, d), w)


def _norm_rows_body(x_ref, g_ref, o_ref):
    x = x_ref[...]
    ms = jnp.mean(x * x, axis=-1, keepdims=True)
    o_ref[...] = x * lax.rsqrt(ms + NORM_EPS) * g_ref[...]


def _norm_rows(x2d, g):
    n, d = x2d.shape
    return pl.pallas_call(
        _norm_rows_body,
        out_shape=jax.ShapeDtypeStruct((n, d), F32),
        name="norm_rows",
    )(x2d, g.reshape(1, d))


def _softplus(x):
    return jnp.maximum(x, 0.0) + jnp.log1p(jnp.exp(-jnp.abs(x)))


def _rwkv_prep_body(p_ref, prev_ref, mu_ref, w0_ref, w2_ref, a0_ref, a2_ref, kk_ref, ka_ref,
                    rk_ref, e_ref, r_ref, w_ref, k_ref, v_ref, rem_ref, rep_ref, sz_ref, bz_ref,
                    carry_ref, *, tt):
    @pl.when(pl.program_id(1) == 0)
    def _():
        carry_ref[...] = prev_ref[...]

    p = p_ref[...]
    rolled = pltpu.roll(p, 1, axis=0)
    row = lax.broadcasted_iota(jnp.int32, p.shape, 0)
    shifted = jnp.where(row == 0, carry_ref[...], rolled)
    carry_ref[...] = p[tt - 1:tt, :]
    h = p + (shifted - p) * mu_ref[...]

    r = h[:, 0:512]
    k = h[:, 512:1024]
    v = h[:, 1024:1536]
    wd = h[:, 1536:1600]
    ad = h[:, 1600:1664]
    z = h[:, 1664:2176]

    lw = w0_ref[...] + jnp.dot(jnp.tanh(wd).astype(BF16), w2_ref[...], preferred_element_type=F32)
    log_decay = -jnp.exp(-_softplus(-lw) - 0.5)
    a = jax.nn.sigmoid(a0_ref[...] + jnp.dot(ad.astype(BF16), a2_ref[...],
                                              preferred_element_type=F32))
    kk = k * kk_ref[...]
    ss = _head_sum(kk * kk, e_ref)
    kk = kk / jnp.maximum(jnp.sqrt(ss), 1e-12)
    kmod = k * (1.0 + (a - 1.0) * ka_ref[...])
    bonus = _head_sum(r * kmod * rk_ref[...], e_ref) * v
    sz = z * jax.nn.sigmoid(z)

    r_ref[...] = r
    w_ref[...] = log_decay
    k_ref[...] = kmod
    v_ref[...] = v
    rem_ref[...] = -kk
    rep_ref[...] = kk * a
    sz_ref[...] = sz
    bz_ref[...] = bonus * sz


def _rwkv_prep(p3, prev, mu, w0, w2, a0, a2, k_k, k_a, r_k, e, tt):
    b, t, _ = p3.shape
    row = lambda a: a.reshape(1, -1)
    seq_spec = pl.BlockSpec((None, tt, RW_WIDTH), lambda i, j: (i, j, 0))
    full = lambda shape: pl.BlockSpec(shape, lambda i, j: (0,) * len(shape))
    out = jax.ShapeDtypeStruct((b, t, RW_WIDTH), F32)
    return pl.pallas_call(
        functools.partial(_rwkv_prep_body, tt=tt),
        out_shape=(out,) * 8,
        grid=(b, t // tt),
        in_specs=[pl.BlockSpec((None, tt, RW_COLS), lambda i, j: (i, j, 0)),
                  pl.BlockSpec((None, 1, RW_COLS), lambda i, j: (i, 0, 0)),
                  full((1, RW_COLS)), full((1, RW_WIDTH)), full((LORA, RW_WIDTH)),
                  full((1, RW_WIDTH)), full((LORA, RW_WIDTH)), full((1, RW_WIDTH)),
                  full((1, RW_WIDTH)), full((1, RW_WIDTH)), full((RW_WIDTH, RW_WIDTH))],
        out_specs=(seq_spec,) * 8,
        scratch_shapes=[pltpu.VMEM((1, RW_COLS), F32)],
        compiler_params=_cparams(("arbitrary", "arbitrary")),
        name="rwkv_prep",
    )(p3, prev, row(mu), row(w0), w2.astype(BF16), row(a0), a2.astype(BF16), row(k_k), row(k_a),
      row(r_k), e)


PAIR = 2 * RW_HEAD
N_PAIRS = RW_HEADS // 2
SCAN_CHUNK = 64
INV_BASE = 8


def _mm(a, b):
    return jnp.dot(a.astype(BF16), b.astype(BF16), preferred_element_type=F32)


def _seg_cumsum(x, seg, reverse=False):
    n = x.shape[0]
    t = lax.rem(lax.broadcasted_iota(jnp.int32, x.shape, 0), seg)
    y = x
    s = 1
    while s < seg:
        if reverse:
            y = y + jnp.where(t < seg - s, pltpu.roll(y, n - s, axis=0), 0.0)
        else:
            y = y + jnp.where(t >= s, pltpu.roll(y, s, axis=0), 0.0)
        s *= 2
    return y - x if reverse else y


def _rwkv_chunk_scan_body(r_ref, lw_ref, k_ref, v_ref, a_ref, b_ref, s0_ref, y_ref, st_ref, h_ref,
                          *, ngrp, nseq, chunk, t_valid):
    c = pl.program_id(1)
    rows = nseq * chunk
    n2 = 2 * rows
    lane = lax.broadcasted_iota(jnp.int32, (rows, PAIR), 1)
    lo_half = lane < RW_HEAD

    @pl.when(c == 0)
    def _():
        z = jnp.zeros((RW_HEAD, RW_HEAD), F32)
        for s in range(ngrp * nseq):
            for p in range(N_PAIRS):
                blk = jnp.concatenate([jnp.concatenate([s0_ref[s, 2 * p], z], axis=1),
                                       jnp.concatenate([z, s0_ref[s, 2 * p + 1]], axis=1)], axis=0)
                h_ref[s, p] = blk.T

    tpos = c * chunk + lax.rem(lax.broadcasted_iota(jnp.int32, (rows, RW_WIDTH), 0), chunk)
    valid = tpos < t_valid
    sls = [slice(p * PAIR, (p + 1) * PAIR) for p in range(N_PAIRS)]
    stack = lambda x: [jnp.concatenate([jnp.where(lo_half, x[:, sl], 0.0),
                                        jnp.where(lo_half, 0.0, x[:, sl])], axis=0) for sl in sls]
    a_st, r_st, b_st, k_st, bh_st, kh_st, v_st, e_pos = [], [], [], [], [], [], [], []
    for grp in range(ngrp):
        seqs = slice(grp * nseq, (grp + 1) * nseq)
        msk = lambda ref: jnp.where(valid, ref[seqs].reshape(rows, RW_WIDTH), 0.0)
        r, lw, k, v, a, b = (msk(x) for x in (r_ref, lw_ref, k_ref, v_ref, a_ref, b_ref))
        g = _seg_cumsum(lw, chunk)
        e_neg = jnp.exp(-g)
        e_end = jnp.exp(_seg_cumsum(lw, chunk, reverse=True))
        e_pos.append(jnp.exp(g))
        a_st += stack(a * jnp.exp(g - lw))
        r_st += stack(r * e_pos[grp])
        b_st += stack(b * e_neg)
        k_st += stack(k * e_neg)
        bh_st += stack(b * e_end)
        kh_st += stack(k * e_end)
        v_st += stack(v)

    row2 = lax.broadcasted_iota(jnp.int32, (n2, n2), 0)
    col2 = lax.broadcasted_iota(jnp.int32, (n2, n2), 1)
    same_seq = (row2 // chunk) == (col2 // chunk)
    strict = same_seq & (row2 > col2)
    incl = same_seq & (row2 >= col2)
    zeros = jnp.zeros((n2, PAIR), F32)
    seq_of_row = lax.rem(lax.broadcasted_iota(jnp.int32, (n2, PAIR), 0), rows) // chunk
    seq_of_col = lax.rem(lax.broadcasted_iota(jnp.int32, (PAIR, 2 * n2), 1), rows) // chunk

    probs = range(ngrp * N_PAIRS)
    each = lambda f, *xs: [f(*(x[i] for x in xs)) for i in probs]
    gram = each(lambda ar, bk: lax.dot_general(ar.astype(BF16), bk.astype(BF16), _QK_DIMS,
                                               preferred_element_type=F32),
                each(lambda x, y: jnp.concatenate([x, y], axis=0), a_st, r_st),
                each(lambda x, y: jnp.concatenate([x, y], axis=0), b_st, k_st))
    mab = each(lambda g_: jnp.where(strict, g_[:n2, :n2], 0.0), gram)
    mak = each(lambda g_: jnp.where(strict, g_[:n2, n2:], 0.0), gram)
    mrbk = each(lambda g_: jnp.concatenate([jnp.where(incl, g_[n2:, :n2], 0.0),
                                            jnp.where(incl, g_[n2:, n2:], 0.0)], axis=1), gram)
    x1 = each(_mm, mak, v_st)

    same = lambda w: (row2 // w) == (col2 // w)
    md = each(lambda m: jnp.where(same(INV_BASE), m, 0.0), mab)
    m2 = each(_mm, md, md)
    m4 = each(_mm, m2, m2)
    t_inv = each(lambda m: (row2 == col2).astype(F32) + m, md)
    t_inv = each(lambda t, m: t + _mm(t, m), t_inv, m2)
    t_inv = each(lambda t, m: t + _mm(t, m), t_inv, m4)
    width = INV_BASE
    while width < chunk:
        off = each(lambda m: jnp.where(same(2 * width) & jnp.logical_not(same(width)), m, 0.0), mab)
        t_off = each(_mm, t_inv, off)
        t_inv = each(lambda t, to: t + _mm(to, t), t_inv, t_off)
        width *= 2

    tx = each(lambda t, x, a_: _mm(t, jnp.concatenate([x, a_], axis=1)), t_inv, x1, a_st)
    rh = each(lambda t, v_: jnp.concatenate(
        [jnp.concatenate([t[:, PAIR:], t[:, :PAIR]], axis=1),
         jnp.concatenate([zeros, v_], axis=1)], axis=0), tx, v_st)
    qy = each(_mm, mrbk, rh)
    q = each(lambda r_, qy_: r_ + qy_[:, :PAIR], r_st, qy)
    yst = each(lambda qy_: qy_[:, PAIR:], qy)
    bk_t = each(lambda x, y: jnp.concatenate([x, y], axis=0).T, bh_st, kh_st)
    only = nseq > 1
    for s in range(nseq):
        slot = lambda i: (i // N_PAIRS * nseq + s, i % N_PAIRS)
        h = [h_ref[slot(i)] for i in probs]
        yst = each(lambda y_, q_, h_: y_ + _mm(jnp.where(seq_of_row == s, q_, 0.0) if only else q_,
                                               h_), yst, q, h)
        gc = each(lambda b_, r_: _mm(jnp.where(seq_of_col == s, b_, 0.0) if only else b_, r_),
                  bk_t, rh)
        gh = each(lambda g_, h_: _mm(g_[:, :PAIR], h_), gc, h)
        last = s * chunk + chunk - 1
        for i in probs:
            g_end = e_pos[i // N_PAIRS][last:last + 1, sls[i % N_PAIRS]]
            g_col = jnp.broadcast_to(g_end, (PAIR, PAIR)).T
            h_ref[slot(i)] = g_col * h[i] + gh[i] + gc[i][:, PAIR:]
    for grp in range(ngrp):
        y_grp = [y_[:rows] + y_[rows:] for y_ in yst[grp * N_PAIRS:(grp + 1) * N_PAIRS]]
        y_ref[grp * nseq:(grp + 1) * nseq] = jnp.concatenate(y_grp, axis=1).reshape(
            nseq, chunk, RW_WIDTH)

    @pl.when(c == pl.num_programs(1) - 1)
    def _():
        for s in range(ngrp * nseq):
            for p in range(N_PAIRS):
                ht = h_ref[s, p].T
                st_ref[s, 2 * p] = ht[:RW_HEAD, :RW_HEAD]
                st_ref[s, 2 * p + 1] = ht[RW_HEAD:, RW_HEAD:]


def _rwkv_chunk_scan(r, lw, k, v, a, b, s0, t_valid):
    bsz, t, _ = r.shape
    chunk = min(t, SCAN_CHUNK)
    nseq = SCAN_CHUNK // chunk
    ngrp = 2
    nblk = ngrp * nseq
    seq = pl.BlockSpec((nblk, chunk, RW_WIDTH), lambda i, j: (i, j, 0))
    st = pl.BlockSpec((nblk, RW_HEADS, RW_HEAD, RW_HEAD), lambda i, j: (i, 0, 0, 0))
    return pl.pallas_call(
        functools.partial(_rwkv_chunk_scan_body, ngrp=ngrp, nseq=nseq, chunk=chunk,
                          t_valid=t_valid),
        out_shape=(jax.ShapeDtypeStruct((bsz, t, RW_WIDTH), F32),
                   jax.ShapeDtypeStruct((bsz, RW_HEADS, RW_HEAD, RW_HEAD), F32)),
        grid=(bsz // nblk, t // chunk),
        in_specs=[seq] * 6 + [st],
        out_specs=(seq, st),
        scratch_shapes=[pltpu.VMEM((nblk, N_PAIRS, PAIR, PAIR), F32)],
        compiler_params=_cparams(("arbitrary", "arbitrary")),
        name="rwkv_chunk_scan",
    )(r, lw, k, v, a, b, s0)


def _da_prep_body(p_ref, qg_ref, kg_ref, e_ref, q_ref, k_ref, *kv_refs, tm):
    q = p_ref[:, 0:512]
    k = p_ref[:, 512:1024]
    qm = _head_sum(q * q, e_ref) * (1.0 / DA_QK)
    km = _head_sum(k * k, e_ref) * (1.0 / DA_QK)
    qn = q * lax.rsqrt(qm + NORM_EPS) * qg_ref[...]
    kn = k * lax.rsqrt(km + NORM_EPS) * kg_ref[...]
    q_ref[...] = (qn * (DA_QK ** -0.5)).astype(BF16)
    k_ref[...] = kn
    if kv_refs:
        kb_ref, vt_ref = kv_refs
        kb_ref[...] = kn.astype(BF16)
        for j in range(tm // SEQ_TILE):
            vt_ref[j] = p_ref[j * SEQ_TILE:(j + 1) * SEQ_TILE, 1024:1536].T.astype(BF16)


def _da_prep(pda, qg, kg, e, tm, with_kv):
    n = pda.shape[0]
    tile8 = lambda g: jnp.tile(g, DA_WIDTH // DA_QK).reshape(1, DA_WIDTH)
    rows = pl.BlockSpec((tm, DA_WIDTH), lambda i: (i, 0))
    bf = jax.ShapeDtypeStruct((n, DA_WIDTH), BF16)
    out_shape = (bf, jax.ShapeDtypeStruct((n, DA_WIDTH), F32))
    out_specs = (rows, rows)
    if with_kv:
        out_shape += (bf, jax.ShapeDtypeStruct((n // SEQ_TILE, DA_WIDTH, SEQ_TILE), BF16))
        out_specs += (rows, pl.BlockSpec((tm // SEQ_TILE, DA_WIDTH, SEQ_TILE), lambda i: (i, 0, 0)))
    return pl.pallas_call(
        functools.partial(_da_prep_body, tm=tm),
        out_shape=out_shape,
        grid=(n // tm,),
        in_specs=[pl.BlockSpec((tm, DA_COLS), lambda i: (i, 0)),
                  pl.BlockSpec((1, DA_WIDTH), lambda i: (0, 0)),
                  pl.BlockSpec((1, DA_WIDTH), lambda i: (0, 0)),
                  pl.BlockSpec((RW_WIDTH, RW_WIDTH), lambda i: (0, 0))],
        out_specs=out_specs,
        compiler_params=_cparams(("arbitrary",)),
        name="da_prep",
    )(pda, tile8(qg), tile8(kg), e)


def _bucket_starts():
    max_exact = N_BUCKETS // 2
    n = np.arange(1, MAX_DISTANCE + 1)
    val = (np.log(n.astype(np.float32) / np.float32(max_exact))
           / np.float32(math.log(MAX_DISTANCE / max_exact)) * np.float32(N_BUCKETS - max_exact))
    inner = (n > max_exact) & (n < MAX_DISTANCE)
    assert np.all(np.abs(val[inner] - np.round(val[inner])) > 1e-3)
    bucket = np.where(n < max_exact, n, np.minimum(max_exact + val.astype(np.int32), N_BUCKETS - 1))
    return [int(n[np.argmax(bucket >= b)]) for b in range(1, N_BUCKETS)]


def _bias_tiles(n, rel_ref):
    accs = [jnp.full(n.shape, rel_ref[0, h], F32) for h in range(DA_HEADS)]
    for b, start in enumerate(_bucket_starts(), start=1):
        in_or_past = n >= start
        accs = [jnp.where(in_or_past, rel_ref[b, h], acc) for h, acc in enumerate(accs)]
    return [jnp.where(n >= 0, acc, NEG_INF) for acc in accs]


def _lambda(lam_ref, layer):
    lq1 = lam_ref[0:1, :]
    lk1 = lam_ref[1:2, :]
    lq2 = lam_ref[2:3, :]
    lk2 = lam_ref[3:4, :]
    li = 0.8 - 0.6 * math.exp(-0.3 * layer)
    lam = (jnp.exp(jnp.sum(lq1 * lk1, axis=1, keepdims=True))
           - jnp.exp(jnp.sum(lq2 * lk2, axis=1, keepdims=True)) + li)
    return lam, li


def _softmax_update(carries, q_sts, kbs, vts, biases):
    n = len(q_sts)
    score = lambda h: (lax.dot_general(kbs[h], q_sts[h], _QK_DIMS, preferred_element_type=F32)
                       + biases[h])
    out = []
    s_next = score(0)
    for h in range(n):
        s = s_next
        if h + 1 < n:
            s_next = score(h + 1)
        m, l, acc = carries[h]
        m_new = jnp.maximum(m, jnp.max(s, axis=0, keepdims=True))
        alpha = jnp.exp(m - m_new)
        p = jnp.exp(s - m_new)
        l = alpha * l + jnp.sum(p, axis=0, keepdims=True)
        acc = alpha * acc + jnp.dot(vts[h], p.astype(BF16), preferred_element_type=F32)
        out.append((m_new, l, acc))
    return tuple(out)


def _head_out(o_st, z, lam, li, sg_ref, tq):
    o = o_st[:tq] - lam * o_st[tq:]
    ms = jnp.mean(o * o, axis=1, keepdims=True)
    return o * lax.rsqrt(ms + NORM_EPS) * sg_ref[...] * (1.0 - li) * (z * jax.nn.sigmoid(z))


def _split_maps(qh):
    lane = lax.broadcasted_iota(jnp.int32, qh.shape, 1)
    zero = jnp.zeros_like(qh)
    return jnp.where(lane < DA_QK, qh, zero), jnp.where(lane >= DA_QK, qh, zero)


_QK_DIMS = (((1,), (1,)), ((), ()))


def _attn_prompt_body(q_ref, k_ref, vt_ref, z_ref, rel_ref, lam_ref, sg_ref, o_ref, bias_ref,
                      *, tq, layer):
    qi = pl.program_id(1)

    @pl.when((pl.program_id(0) == 0) & (qi == 0))
    def _():
        d = (lax.broadcasted_iota(jnp.int32, (tq, tq), 1)
             - lax.broadcasted_iota(jnp.int32, (tq, tq), 0))
        for blk in range(2):
            tiles = _bias_tiles(d + blk * tq, rel_ref)
            for h in range(DA_HEADS):
                bias_ref[blk * DA_HEADS + h] = tiles[h]

    lam, li = _lambda(lam_ref, layer)
    n_far = jnp.maximum(qi - 1, 0)
    prev_pen = jnp.where(qi >= 1, 0.0, NEG_INF)
    heads = range(DA_HEADS)
    sls = [slice(h * DA_V, (h + 1) * DA_V) for h in heads]
    q_sts = [jnp.concatenate(_split_maps(q_ref[:, sl]), axis=0) for sl in sls]

    def attend(carries, kblk, biases):
        off = pl.multiple_of(kblk * tq, tq)
        return _softmax_update(carries, q_sts, [k_ref[pl.ds(off, tq), sl] for sl in sls],
                               [vt_ref[kblk, sl, :] for sl in sls], biases)

    far_bias = [rel_ref[N_BUCKETS - 1, h] for h in heads]
    init = (jnp.full((1, 2 * tq), NEG_INF, F32), jnp.zeros((1, 2 * tq), F32),
            jnp.zeros((DA_V, 2 * tq), F32))
    carries = lax.fori_loop(0, n_far, lambda ki, c: attend(c, ki, far_bias), (init,) * DA_HEADS)
    twice = lambda x: jnp.concatenate([x, x], axis=1)
    carries = attend(carries, n_far, [twice(bias_ref[DA_HEADS + h] + prev_pen) for h in heads])
    carries = attend(carries, qi, [twice(bias_ref[h]) for h in heads])
    for h in heads:
        _, l, acc = carries[h]
        o_t = acc / l
        o_st = jnp.concatenate([o_t[:, :tq].T, o_t[:, tq:].T], axis=0)
        o_ref[:, sls[h]] = _head_out(o_st, z_ref[:, sls[h]], lam, li, sg_ref, tq)


def _attn_prompt(qn, kb, vt, pda, rel_bias, lam, sg, layer, tq):
    b, t, _ = qn.shape
    nblk = t // tq
    return pl.pallas_call(
        functools.partial(_attn_prompt_body, tq=tq, layer=layer),
        out_shape=jax.ShapeDtypeStruct((b, t, DA_WIDTH), F32),
        grid=(b, nblk),
        in_specs=[pl.BlockSpec((None, tq, DA_WIDTH), lambda i, q: (i, q, 0)),
                  pl.BlockSpec((None, t, DA_WIDTH), lambda i, q: (i, 0, 0)),
                  pl.BlockSpec((nblk, DA_WIDTH, tq), lambda i, q: (i, 0, 0)),
                  pl.BlockSpec((None, tq, DA_WIDTH), lambda i, q: (i, q, 3)),
                  pl.BlockSpec(memory_space=pltpu.SMEM),
                  pl.BlockSpec((4, DA_QK), lambda i, q: (0, 0)),
                  pl.BlockSpec((1, DA_V), lambda i, q: (0, 0))],
        out_specs=pl.BlockSpec((None, tq, DA_WIDTH), lambda i, q: (i, q, 0)),
        scratch_shapes=[pltpu.VMEM((2 * DA_HEADS, tq, tq), F32)],
        compiler_params=_cparams(("arbitrary", "arbitrary")),
        name="attn_prompt",
    )(qn, kb, vt, pda, rel_bias, lam, sg.reshape(1, DA_V))


def _attn_sample_body(pt_ref, q_ref, *refs, tq, npg, layer):
    kc = refs[:npg]
    vc = refs[npg:2 * npg]
    kn_ref, vn_ref, z_ref, rel_ref, lam_ref, sg_ref, o_ref, bias_ref = refs[2 * npg:]
    past = npg * PAGE_SIZE
    ncol = past + PAGE_SIZE

    @pl.when(pl.program_id(0) == 0)
    def _():
        t_idx = lax.rem(lax.broadcasted_iota(jnp.int32, (2 * tq, ncol), 0), tq)
        col = lax.broadcasted_iota(jnp.int32, (2 * tq, ncol), 1)
        tiles = _bias_tiles(past + t_idx - col, rel_ref)
        for h in range(DA_HEADS):
            bias_ref[h] = tiles[h]

    lam, li = _lambda(lam_ref, layer)
    pad = jnp.zeros((PAGE_SIZE - tq, DA_V), F32)
    outs = []
    for h in range(DA_HEADS):
        sl = slice(h * DA_V, (h + 1) * DA_V)
        q_st = jnp.concatenate(_split_maps(q_ref[:, sl]), axis=0)
        rows_h = pl.ds(h, PAGE_SIZE, stride=DA_HEADS)
        page_h = lambda ref: ref.reshape(PAGE_SIZE * DA_HEADS, DA_V)[rows_h, :]
        kh = jnp.concatenate([page_h(kc[j]) for j in range(npg)] + [kn_ref[:, sl], pad], axis=0)
        vh = jnp.concatenate([page_h(vc[j]) for j in range(npg)] + [vn_ref[:, sl], pad], axis=0)
        s = lax.dot_general(q_st, kh.astype(BF16), _QK_DIMS, preferred_element_type=F32)
        s = s + bias_ref[h]
        m = jnp.max(s, axis=1, keepdims=True)
        p = jnp.exp(s - m)
        l = jnp.sum(p, axis=1, keepdims=True)
        o = jnp.dot(p.astype(BF16), vh.astype(BF16), preferred_element_type=F32) / l
        outs.append(_head_out(o, z_ref[:, sl], lam, li, sg_ref, tq))
    o_ref[...] = jnp.concatenate(outs, axis=1)


def _attn_sample(qn, kn, pda, cache_k, cache_v, page_table, rel_bias, lam, sg, layer):
    b, tq, _ = qn.shape
    npg = page_table.shape[1]
    page_specs = [pl.BlockSpec((None, None, PAGE_SIZE, DA_HEADS, DA_V),
                               lambda i, pt, j=j: (layer, pt[i, j], 0, 0, 0)) for j in range(npg)]
    seq_spec = lambda col: pl.BlockSpec((None, tq, DA_WIDTH), lambda i, pt: (i, 0, col))
    grid_spec = pltpu.PrefetchScalarGridSpec(
        num_scalar_prefetch=1,
        grid=(b,),
        in_specs=[seq_spec(0)] + page_specs + page_specs
                 + [seq_spec(0), seq_spec(2), seq_spec(3),
                    pl.BlockSpec(memory_space=pltpu.SMEM),
                    pl.BlockSpec((4, DA_QK), lambda i, pt: (0, 0)),
                    pl.BlockSpec((1, DA_V), lambda i, pt: (0, 0))],
        out_specs=pl.BlockSpec((None, tq, DA_WIDTH), lambda i, pt: (i, 0, 0)),
        scratch_shapes=[pltpu.VMEM((DA_HEADS, 2 * tq, (npg + 1) * PAGE_SIZE), F32)],
    )
    return pl.pallas_call(
        functools.partial(_attn_sample_body, tq=tq, npg=npg, layer=layer),
        out_shape=jax.ShapeDtypeStruct((b, tq, DA_WIDTH), F32),
        grid_spec=grid_spec,
        compiler_params=_cparams(("arbitrary",)),
        name="attn_sample",
    )(page_table, qn, *([cache_k] * npg), *([cache_v] * npg), kn, pda, pda, rel_bias, lam,
      sg.reshape(1, DA_V))


def _out_proj_body(x_ref, y_ref, sz_ref, bz_ref, yb_ref, g_ref, lg_ref, lb_ref, e_ref,
                   wa_ref, wb_ref, wo_ref, o_ref):
    y = y_ref[...]
    mean = _head_sum(y, e_ref) * (1.0 / RW_HEAD)
    yc = y - mean
    var = _head_sum(yc * yc, e_ref) * (1.0 / RW_HEAD)
    yn = yc * lax.rsqrt(var + GN_EPS) * lg_ref[...] + lb_ref[...]
    ya = yn * sz_ref[...] + bz_ref[...]
    ga = jax.nn.sigmoid(g_ref[:, 0:D_MODEL])
    gb = jax.nn.sigmoid(g_ref[:, D_MODEL:2 * D_MODEL])
    m = (ga * jnp.dot(ya.astype(BF16), wa_ref[...], preferred_element_type=F32)
         + gb * jnp.dot(yb_ref[...].astype(BF16), wb_ref[...], preferred_element_type=F32))
    o_ref[...] = x_ref[...] + jnp.dot(m.astype(BF16), wo_ref[...], preferred_element_type=F32)


def _out_proj(x2d, y, sz, bz, yb, pg, lnx_g, lnx_b, e, wa, wb, wo, tm):
    n = x2d.shape[0]
    rows = lambda c: pl.BlockSpec((tm, c), lambda i: (i, 0))
    full = lambda r, c: pl.BlockSpec((r, c), lambda i: (0, 0))
    return pl.pallas_call(
        _out_proj_body,
        out_shape=jax.ShapeDtypeStruct((n, D_MODEL), F32),
        grid=(n // tm,),
        in_specs=[rows(D_MODEL), rows(RW_WIDTH), rows(RW_WIDTH), rows(RW_WIDTH), rows(DA_WIDTH),
                  rows(G_COLS), full(1, RW_WIDTH), full(1, RW_WIDTH), full(RW_WIDTH, RW_WIDTH),
                  full(RW_WIDTH, D_MODEL), full(DA_WIDTH, D_MODEL), full(D_MODEL, D_MODEL)],
        out_specs=rows(D_MODEL),
        compiler_params=_cparams(("arbitrary",)),
        name="out_proj",
    )(x2d, y, sz, bz, yb, pg, lnx_g.reshape(1, -1), lnx_b.reshape(1, -1), e, wa, wb, wo)


def _trunk(x, t_valid, tm, tt, shift0, state0, cache_k, cache_v, page_table, prm):
    b, t, _ = x.shape
    n = b * t
    e = (jnp.arange(RW_WIDTH)[:, None] // RW_HEAD == jnp.arange(RW_WIDTH)[None, :] // RW_HEAD)
    e = e.astype(BF16)
    new_k, new_v, new_s, new_shift = [], [], [], []
    for l in range(2):
        w_in = prm["w_in"][l].astype(BF16)
        w_rw = w_in[:, :RW_COLS]
        x2d = x.reshape(n, D_MODEL)
        p_rw = _in_proj(x2d, prm["norm_g"][l], w_rw, tm)
        pda = _in_proj(x2d, prm["norm_g"][l], w_in[:, RW_COLS:RW_COLS + DA_COLS], tm)
        pg = _in_proj(x2d, prm["norm_g"][l], w_in[:, RW_COLS + DA_COLS:], tm)
        new_shift.append(_norm_rows(x[:, t_valid - 1], prm["norm_g"][l]))

        if shift0 is None:
            prev = jnp.zeros((b, 1, RW_COLS), F32)
            s_init = jnp.zeros((b, RW_HEADS, RW_HEAD, RW_HEAD), F32)
        else:
            prev = _in_proj(shift0[l], prm["norm_g"][l], w_rw, shift0[l].shape[0],
                            normalize=False).reshape(b, 1, RW_COLS)
            s_init = state0[l]
        r, w, k, v, rem, rep, sz, bz = _rwkv_prep(
            p_rw.reshape(b, t, RW_COLS), prev, prm["mu_shift"][l], prm["w0"][l], prm["w2"][l],
            prm["a0"][l], prm["a2"][l], prm["k_k"][l], prm["k_a"][l], prm["r_k"][l], e, tt)
        y, s_fin = _rwkv_chunk_scan(r, w, k, v, rem, rep, s_init, t_valid)

        qn, kn, *kv = _da_prep(pda, prm["q_norm_g"][l], prm["k_norm_g"][l], e, tm, cache_k is None)
        lam = jnp.stack([prm["lam_q1"][l], prm["lam_k1"][l], prm["lam_q2"][l], prm["lam_k2"][l]])
        pda3 = pda.reshape(b, t, DA_COLS)
        qn3 = qn.reshape(b, t, DA_WIDTH)
        kn3 = kn.reshape(b, t, DA_WIDTH)
        if cache_k is None:
            yb = _attn_prompt(qn3, kv[0].reshape(b, t, DA_WIDTH), kv[1], pda3, prm["rel_bias"], lam,
                              prm["subln_g"][l], l, tt)
        else:
            yb = _attn_sample(qn3, kn3, pda3, cache_k, cache_v, page_table, prm["rel_bias"], lam,
                              prm["subln_g"][l], l)

        x = _out_proj(x2d, y.reshape(n, RW_WIDTH), sz.reshape(n, RW_WIDTH),
                      bz.reshape(n, RW_WIDTH), yb.reshape(n, DA_WIDTH), pg, prm["lnx_g"][l],
                      prm["lnx_b"][l], e, prm["w_a_out"][l].astype(BF16),
                      prm["w_b_out"][l].astype(BF16), prm["w_o"][l].astype(BF16),
                      tm).reshape(b, t, D_MODEL)
        new_k.append(kn3[:, :t_valid].reshape(b, t_valid, DA_HEADS, DA_V))
        new_v.append(pda3[:, :t_valid, 2 * DA_WIDTH:3 * DA_WIDTH].reshape(b, t_valid, DA_HEADS, DA_V))
        new_s.append(s_fin)
    return x, jnp.stack(new_k), jnp.stack(new_v), jnp.stack(new_s), jnp.stack(new_shift)


def kernel(x_prompt, x_sample, cache_k, cache_v, page_table, state_rwkv, state_shift, meta_tokens,
           rel_bias, norm_g, w_in, mu_shift, w0, w2, a0, a2, k_k, k_a, r_k, lnx_g, lnx_b,
           q_norm_g, k_norm_g, lam_q1, lam_k1, lam_q2, lam_k2, subln_g, w_a_out, w_b_out, w_o):
    prm = dict(rel_bias=rel_bias, norm_g=norm_g, w_in=w_in, mu_shift=mu_shift, w0=w0, w2=w2,
               a0=a0, a2=a2, k_k=k_k, k_a=k_a, r_k=r_k.reshape(2, RW_WIDTH), lnx_g=lnx_g,
               lnx_b=lnx_b, q_norm_g=q_norm_g, k_norm_g=k_norm_g, lam_q1=lam_q1, lam_k1=lam_k1,
               lam_q2=lam_q2, lam_k2=lam_k2, subln_g=subln_g, w_a_out=w_a_out, w_b_out=w_b_out,
               w_o=w_o)
    bp, seq, _ = x_prompt.shape
    t_valid = seq + N_META
    t_pad = -(-t_valid // SEQ_TILE) * SEQ_TILE
    meta = jnp.broadcast_to(meta_tokens[None], (bp, N_META, D_MODEL))
    xp = jnp.concatenate([meta, x_prompt, jnp.zeros((bp, t_pad - t_valid, D_MODEL), F32)], axis=1)
    yp, kp, vp, sp, hp = _trunk(xp, t_valid, 512, SEQ_TILE, None, None, None, None, None, prm)

    ts = x_sample.shape[1]
    ys, k_s, v_s, s_s, h_s = _trunk(x_sample, ts, 512, ts, state_shift, state_rwkv, cache_k,
                                    cache_v, page_table, prm)
    return (yp[:, N_META:t_valid], ys, kp, vp, sp, hp, k_s, v_s, s_s, h_s)
```

```python
import functools
import math

import jax
import jax.numpy as jnp
import numpy as np
from jax import lax
from jax.experimental import pallas as pl
from jax.experimental.pallas import tpu as pltpu

F32 = jnp.float32
BF16 = jnp.bfloat16

D_MODEL = 1024
N_META = 16
PAGE_SIZE = 128
RW_WIDTH = 512
RW_HEAD = 64
RW_HEADS = 8
LORA = 64
GN_EPS = 64e-5
DA_WIDTH = 512
DA_QK = 64
DA_HEADS = 4
DA_V = 128
N_BUCKETS = 32
MAX_DISTANCE = 128
NORM_EPS = 1e-6
NEG_INF = -1e30

RW_COLS = 3 * RW_WIDTH + 2 * LORA + RW_WIDTH
DA_COLS = 4 * DA_WIDTH
G_COLS = 2 * D_MODEL

SEQ_TILE = 128
VMEM_LIMIT = 48 * 1024 * 1024


def _cparams(sem):
    return pltpu.CompilerParams(dimension_semantics=sem, vmem_limit_bytes=VMEM_LIMIT)


def _head_sum(x, e_ref):
    hi = x.astype(BF16)
    lo = (x - hi.astype(F32)).astype(BF16)
    e = e_ref[...]
    return (jnp.dot(hi, e, preferred_element_type=F32)
            + jnp.dot(lo, e, preferred_element_type=F32))


def _in_proj_body(x_ref, g_ref, w_ref, o_ref, *, normalize):
    x = x_ref[...]
    if normalize:
        ms = jnp.mean(x * x, axis=-1, keepdims=True)
        x = x * lax.rsqrt(ms + NORM_EPS) * g_ref[...]
    o_ref[...] = jnp.dot(x.astype(BF16), w_ref[...], preferred_element_type=F32)


def _in_proj(x2d, g, w, tm, normalize=True):
    n, d = x2d.shape
    c = w.shape[1]
    return pl.pallas_call(
        functools.partial(_in_proj_body, normalize=normalize),
        out_shape=jax.ShapeDtypeStruct((n, c), F32),
        grid=(n // tm,),
        in_specs=[pl.BlockSpec((tm, d), lambda i: (i, 0)),
                  pl.BlockSpec((1, d), lambda i: (0, 0)),
                  pl.BlockSpec((d, c), lambda i: (0, 0))],
        out_specs=pl.BlockSpec((tm, c), lambda i: (i, 0)),
        compiler_params=_cparams(("arbitrary",)),
        name="in_proj",
    )(x2d, g.reshape(1, d), w)


def _norm_rows_body(x_ref, g_ref, o_ref):
    x = x_ref[...]
    ms = jnp.mean(x * x, axis=-1, keepdims=True)
    o_ref[...] = x * lax.rsqrt(ms + NORM_EPS) * g_ref[...]


def _norm_rows(x2d, g):
    n, d = x2d.shape
    return pl.pallas_call(
        _norm_rows_body,
        out_shape=jax.ShapeDtypeStruct((n, d), F32),
        name="norm_rows",
    )(x2d, g.reshape(1, d))


def _softplus(x):
    return jnp.maximum(x, 0.0) + jnp.log1p(jnp.exp(-jnp.abs(x)))


def _rwkv_prep_body(p_ref, prev_ref, mu_ref, w0_ref, w2_ref, a0_ref, a2_ref, kk_ref, ka_ref,
                    rk_ref, e_ref, r_ref, w_ref, k_ref, v_ref, rem_ref, rep_ref, sz_ref, bz_ref,
                    carry_ref, *, tt):
    @pl.when(pl.program_id(1) == 0)
    def _():
        carry_ref[...] = prev_ref[...]

    p = p_ref[...]
    rolled = pltpu.roll(p, 1, axis=0)
    row = lax.broadcasted_iota(jnp.int32, p.shape, 0)
    shifted = jnp.where(row == 0, carry_ref[...], rolled)
    carry_ref[...] = p[tt - 1:tt, :]
    h = p + (shifted - p) * mu_ref[...]

    r = h[:, 0:512]
    k = h[:, 512:1024]
    v = h[:, 1024:1536]
    wd = h[:, 1536:1600]
    ad = h[:, 1600:1664]
    z = h[:, 1664:2176]

    lw = w0_ref[...] + jnp.dot(jnp.tanh(wd).astype(BF16), w2_ref[...], preferred_element_type=F32)
    log_decay = -jnp.exp(-_softplus(-lw) - 0.5)
    a = jax.nn.sigmoid(a0_ref[...] + jnp.dot(ad.astype(BF16), a2_ref[...],
                                              preferred_element_type=F32))
    kk = k * kk_ref[...]
    ss = _head_sum(kk * kk, e_ref)
    kk = kk / jnp.maximum(jnp.sqrt(ss), 1e-12)
    kmod = k * (1.0 + (a - 1.0) * ka_ref[...])
    bonus = _head_sum(r * kmod * rk_ref[...], e_ref) * v
    sz = z * jax.nn.sigmoid(z)

    r_ref[...] = r
    w_ref[...] = log_decay
    k_ref[...] = kmod
    v_ref[...] = v
    rem_ref[...] = -kk
    rep_ref[...] = kk * a
    sz_ref[...] = sz
    bz_ref[...] = bonus * sz


def _rwkv_prep(p3, prev, mu, w0, w2, a0, a2, k_k, k_a, r_k, e, tt):
    b, t, _ = p3.shape
    row = lambda a: a.reshape(1, -1)
    seq_spec = pl.BlockSpec((None, tt, RW_WIDTH), lambda i, j: (i, j, 0))
    full = lambda shape: pl.BlockSpec(shape, lambda i, j: (0,) * len(shape))
    out = jax.ShapeDtypeStruct((b, t, RW_WIDTH), F32)
    return pl.pallas_call(
        functools.partial(_rwkv_prep_body, tt=tt),
        out_shape=(out,) * 8,
        grid=(b, t // tt),
        in_specs=[pl.BlockSpec((None, tt, RW_COLS), lambda i, j: (i, j, 0)),
                  pl.BlockSpec((None, 1, RW_COLS), lambda i, j: (i, 0, 0)),
                  full((1, RW_COLS)), full((1, RW_WIDTH)), full((LORA, RW_WIDTH)),
                  full((1, RW_WIDTH)), full((LORA, RW_WIDTH)), full((1, RW_WIDTH)),
                  full((1, RW_WIDTH)), full((1, RW_WIDTH)), full((RW_WIDTH, RW_WIDTH))],
        out_specs=(seq_spec,) * 8,
        scratch_shapes=[pltpu.VMEM((1, RW_COLS), F32)],
        compiler_params=_cparams(("arbitrary", "arbitrary")),
        name="rwkv_prep",
    )(p3, prev, row(mu), row(w0), w2.astype(BF16), row(a0), a2.astype(BF16), row(k_k), row(k_a),
      row(r_k), e)


PAIR = 2 * RW_HEAD
N_PAIRS = RW_HEADS // 2
SCAN_CHUNK = 64
SCAN_GROUPS = 4
INV_BASE = 8


def _mm(a, b):
    return jnp.dot(a.astype(BF16), b.astype(BF16), preferred_element_type=F32)


def _seg_cumsum(x, seg, reverse=False):
    n = x.shape[0]
    t = lax.rem(lax.broadcasted_iota(jnp.int32, x.shape, 0), seg)
    y = x
    s = 1
    while s < seg:
        if reverse:
            y = y + jnp.where(t < seg - s, pltpu.roll(y, n - s, axis=0), 0.0)
        else:
            y = y + jnp.where(t >= s, pltpu.roll(y, s, axis=0), 0.0)
        s *= 2
    return y - x if reverse else y


def _rwkv_chunk_scan_body(r_ref, lw_ref, k_ref, v_ref, a_ref, b_ref, s0_ref, y_ref, st_ref, h_ref,
                          *, ngrp, nseq, chunk, t_valid):
    c = pl.program_id(1)
    rows = nseq * chunk
    n2 = 2 * rows
    lane = lax.broadcasted_iota(jnp.int32, (rows, PAIR), 1)
    lo_half = lane < RW_HEAD

    @pl.when(c == 0)
    def _():
        z = jnp.zeros((RW_HEAD, RW_HEAD), F32)
        for s in range(ngrp * nseq):
            for p in range(N_PAIRS):
                blk = jnp.concatenate([jnp.concatenate([s0_ref[s, 2 * p], z], axis=1),
                                       jnp.concatenate([z, s0_ref[s, 2 * p + 1]], axis=1)], axis=0)
                h_ref[s, p] = blk.T

    tpos = c * chunk + lax.rem(lax.broadcasted_iota(jnp.int32, (rows, RW_WIDTH), 0), chunk)
    valid = tpos < t_valid
    sls = [slice(p * PAIR, (p + 1) * PAIR) for p in range(N_PAIRS)]
    stack = lambda x: [jnp.concatenate([jnp.where(lo_half, x[:, sl], 0.0),
                                        jnp.where(lo_half, 0.0, x[:, sl])], axis=0) for sl in sls]
    a_st, r_st, b_st, k_st, bh_st, kh_st, v_st, e_pos = [], [], [], [], [], [], [], []
    for grp in range(ngrp):
        seqs = slice(grp * nseq, (grp + 1) * nseq)
        msk = lambda ref: jnp.where(valid, ref[seqs].reshape(rows, RW_WIDTH), 0.0)
        r, lw, k, v, a, b = (msk(x) for x in (r_ref, lw_ref, k_ref, v_ref, a_ref, b_ref))
        g = _seg_cumsum(lw, chunk)
        e_neg = jnp.exp(-g)
        e_end = jnp.exp(_seg_cumsum(lw, chunk, reverse=True))
        e_pos.append(jnp.exp(g))
        a_st += stack(a * jnp.exp(g - lw))
        r_st += stack(r * e_pos[grp])
        b_st += stack(b * e_neg)
        k_st += stack(k * e_neg)
        bh_st += stack(b * e_end)
        kh_st += stack(k * e_end)
        v_st += stack(v)

    row2 = lax.broadcasted_iota(jnp.int32, (n2, n2), 0)
    col2 = lax.broadcasted_iota(jnp.int32, (n2, n2), 1)
    same_seq = (row2 // chunk) == (col2 // chunk)
    strict = same_seq & (row2 > col2)
    incl = same_seq & (row2 >= col2)
    zeros = jnp.zeros((n2, PAIR), F32)
    seq_of_row = lax.rem(lax.broadcasted_iota(jnp.int32, (n2, PAIR), 0), rows) // chunk
    seq_of_col = lax.rem(lax.broadcasted_iota(jnp.int32, (PAIR, 2 * n2), 1), rows) // chunk

    probs = range(ngrp * N_PAIRS)
    each = lambda f, *xs: [f(*(x[i] for x in xs)) for i in probs]
    gram = each(lambda ar, bk: lax.dot_general(ar.astype(BF16), bk.astype(BF16), _QK_DIMS,
                                               preferred_element_type=F32),
                each(lambda x, y: jnp.concatenate([x, y], axis=0), a_st, r_st),
                each(lambda x, y: jnp.concatenate([x, y], axis=0), b_st, k_st))
    mab = each(lambda g_: jnp.where(strict, g_[:n2, :n2], 0.0), gram)
    mak = each(lambda g_: jnp.where(strict, g_[:n2, n2:], 0.0), gram)
    mrbk = each(lambda g_: jnp.concatenate([jnp.where(incl, g_[n2:, :n2], 0.0),
                                            jnp.where(incl, g_[n2:, n2:], 0.0)], axis=1), gram)
    x1 = each(_mm, mak, v_st)

    same = lambda w: (row2 // w) == (col2 // w)
    md = each(lambda m: jnp.where(same(INV_BASE), m, 0.0), mab)
    m2 = each(_mm, md, md)
    m4 = each(_mm, m2, m2)
    t_inv = each(lambda m: (row2 == col2).astype(F32) + m, md)
    t_inv = each(lambda t, m: t + _mm(t, m), t_inv, m2)
    t_inv = each(lambda t, m: t + _mm(t, m), t_inv, m4)
    width = INV_BASE
    while width < chunk:
        off = each(lambda m: jnp.where(same(2 * width) & jnp.logical_not(same(width)), m, 0.0), mab)
        t_off = each(_mm, t_inv, off)
        t_inv = each(lambda t, to: t + _mm(to, t), t_inv, t_off)
        width *= 2

    tx = each(lambda t, x, a_: _mm(t, jnp.concatenate([x, a_], axis=1)), t_inv, x1, a_st)
    rh = each(lambda t, v_: jnp.concatenate(
        [jnp.concatenate([t[:, PAIR:], t[:, :PAIR]], axis=1),
         jnp.concatenate([zeros, v_], axis=1)], axis=0), tx, v_st)
    qy = each(_mm, mrbk, rh)
    q = each(lambda r_, qy_: r_ + qy_[:, :PAIR], r_st, qy)
    yst = each(lambda qy_: qy_[:, PAIR:], qy)
    bk_t = each(lambda x, y: jnp.concatenate([x, y], axis=0).T, bh_st, kh_st)
    only = nseq > 1
    for s in range(nseq):
        slot = lambda i: (i // N_PAIRS * nseq + s, i % N_PAIRS)
        h = [h_ref[slot(i)] for i in probs]
        yst = each(lambda y_, q_, h_: y_ + _mm(jnp.where(seq_of_row == s, q_, 0.0) if only else q_,
                                               h_), yst, q, h)
        gc = each(lambda b_, r_: _mm(jnp.where(seq_of_col == s, b_, 0.0) if only else b_, r_),
                  bk_t, rh)
        gh = each(lambda g_, h_: _mm(g_[:, :PAIR], h_), gc, h)
        last = s * chunk + chunk - 1
        for i in probs:
            g_end = e_pos[i // N_PAIRS][last:last + 1, sls[i % N_PAIRS]]
            g_col = jnp.broadcast_to(g_end, (PAIR, PAIR)).T
            h_ref[slot(i)] = g_col * h[i] + gh[i] + gc[i][:, PAIR:]
    for grp in range(ngrp):
        y_grp = [y_[:rows] + y_[rows:] for y_ in yst[grp * N_PAIRS:(grp + 1) * N_PAIRS]]
        y_ref[grp * nseq:(grp + 1) * nseq] = jnp.concatenate(y_grp, axis=1).reshape(
            nseq, chunk, RW_WIDTH)

    @pl.when(c == pl.num_programs(1) - 1)
    def _():
        for s in range(ngrp * nseq):
            for p in range(N_PAIRS):
                ht = h_ref[s, p].T
                st_ref[s, 2 * p] = ht[:RW_HEAD, :RW_HEAD]
                st_ref[s, 2 * p + 1] = ht[RW_HEAD:, RW_HEAD:]


def _rwkv_chunk_scan(r, lw, k, v, a, b, s0, t_valid):
    bsz, t, _ = r.shape
    chunk = min(t, SCAN_CHUNK)
    nseq = SCAN_CHUNK // chunk
    ngrp = SCAN_GROUPS if nseq == 1 else 2
    nblk = ngrp * nseq
    seq = pl.BlockSpec((nblk, chunk, RW_WIDTH), lambda i, j: (i, j, 0))
    st = pl.BlockSpec((nblk, RW_HEADS, RW_HEAD, RW_HEAD), lambda i, j: (i, 0, 0, 0))
    return pl.pallas_call(
        functools.partial(_rwkv_chunk_scan_body, ngrp=ngrp, nseq=nseq, chunk=chunk,
                          t_valid=t_valid),
        out_shape=(jax.ShapeDtypeStruct((bsz, t, RW_WIDTH), F32),
                   jax.ShapeDtypeStruct((bsz, RW_HEADS, RW_HEAD, RW_HEAD), F32)),
        grid=(bsz // nblk, t // chunk),
        in_specs=[seq] * 6 + [st],
        out_specs=(seq, st),
        scratch_shapes=[pltpu.VMEM((nblk, N_PAIRS, PAIR, PAIR), F32)],
        compiler_params=_cparams(("arbitrary", "arbitrary")),
        name="rwkv_chunk_scan",
    )(r, lw, k, v, a, b, s0)


def _da_prep_body(p_ref, qg_ref, kg_ref, e_ref, q_ref, k_ref, *kv_refs, tm):
    q = p_ref[:, 0:512]
    k = p_ref[:, 512:1024]
    qm = _head_sum(q * q, e_ref) * (1.0 / DA_QK)
    km = _head_sum(k * k, e_ref) * (1.0 / DA_QK)
    qn = q * lax.rsqrt(qm + NORM_EPS) * qg_ref[...]
    kn = k * lax.rsqrt(km + NORM_EPS) * kg_ref[...]
    q_ref[...] = (qn * (DA_QK ** -0.5)).astype(BF16)
    k_ref[...] = kn
    if kv_refs:
        kb_ref, vt_ref = kv_refs
        kb_ref[...] = kn.astype(BF16)
        for j in range(tm // SEQ_TILE):
            vt_ref[j] = p_ref[j * SEQ_TILE:(j + 1) * SEQ_TILE, 1024:1536].T.astype(BF16)


def _da_prep(pda, qg, kg, e, tm, with_kv):
    n = pda.shape[0]
    tile8 = lambda g: jnp.tile(g, DA_WIDTH // DA_QK).reshape(1, DA_WIDTH)
    rows = pl.BlockSpec((tm, DA_WIDTH), lambda i: (i, 0))
    bf = jax.ShapeDtypeStruct((n, DA_WIDTH), BF16)
    out_shape = (bf, jax.ShapeDtypeStruct((n, DA_WIDTH), F32))
    out_specs = (rows, rows)
    if with_kv:
        out_shape += (bf, jax.ShapeDtypeStruct((n // SEQ_TILE, DA_WIDTH, SEQ_TILE), BF16))
        out_specs += (rows, pl.BlockSpec((tm // SEQ_TILE, DA_WIDTH, SEQ_TILE), lambda i: (i, 0, 0)))
    return pl.pallas_call(
        functools.partial(_da_prep_body, tm=tm),
        out_shape=out_shape,
        grid=(n // tm,),
        in_specs=[pl.BlockSpec((tm, DA_COLS), lambda i: (i, 0)),
                  pl.BlockSpec((1, DA_WIDTH), lambda i: (0, 0)),
                  pl.BlockSpec((1, DA_WIDTH), lambda i: (0, 0)),
                  pl.BlockSpec((RW_WIDTH, RW_WIDTH), lambda i: (0, 0))],
        out_specs=out_specs,
        compiler_params=_cparams(("arbitrary",)),
        name="da_prep",
    )(pda, tile8(qg), tile8(kg), e)


def _bucket_starts():
    max_exact = N_BUCKETS // 2
    n = np.arange(1, MAX_DISTANCE + 1)
    val = (np.log(n.astype(np.float32) / np.float32(max_exact))
           / np.float32(math.log(MAX_DISTANCE / max_exact)) * np.float32(N_BUCKETS - max_exact))
    inner = (n > max_exact) & (n < MAX_DISTANCE)
    assert np.all(np.abs(val[inner] - np.round(val[inner])) > 1e-3)
    bucket = np.where(n < max_exact, n, np.minimum(max_exact + val.astype(np.int32), N_BUCKETS - 1))
    return [int(n[np.argmax(bucket >= b)]) for b in range(1, N_BUCKETS)]


def _bias_tiles(n, rel_ref):
    accs = [jnp.full(n.shape, rel_ref[0, h], F32) for h in range(DA_HEADS)]
    for b, start in enumerate(_bucket_starts(), start=1):
        in_or_past = n >= start
        accs = [jnp.where(in_or_past, rel_ref[b, h], acc) for h, acc in enumerate(accs)]
    return [jnp.where(n >= 0, acc, NEG_INF) for acc in accs]


def _lambda(lam_ref, layer):
    lq1 = lam_ref[0:1, :]
    lk1 = lam_ref[1:2, :]
    lq2 = lam_ref[2:3, :]
    lk2 = lam_ref[3:4, :]
    li = 0.8 - 0.6 * math.exp(-0.3 * layer)
    lam = (jnp.exp(jnp.sum(lq1 * lk1, axis=1, keepdims=True))
           - jnp.exp(jnp.sum(lq2 * lk2, axis=1, keepdims=True)) + li)
    return lam, li


def _softmax_update(carries, q_sts, kbs, vts, biases):
    n = len(q_sts)
    score = lambda h: (lax.dot_general(kbs[h], q_sts[h], _QK_DIMS, preferred_element_type=F32)
                       + biases[h])
    out = []
    s_next = score(0)
    for h in range(n):
        s = s_next
        if h + 1 < n:
            s_next = score(h + 1)
        m, l, acc = carries[h]
        m_new = jnp.maximum(m, jnp.max(s, axis=0, keepdims=True))
        alpha = jnp.exp(m - m_new)
        p = jnp.exp(s - m_new)
        l = alpha * l + jnp.sum(p, axis=0, keepdims=True)
        acc = alpha * acc + jnp.dot(vts[h], p.astype(BF16), preferred_element_type=F32)
        out.append((m_new, l, acc))
    return tuple(out)


def _head_out(o_st, z, lam, li, sg_ref, tq):
    o = o_st[:tq] - lam * o_st[tq:]
    ms = jnp.mean(o * o, axis=1, keepdims=True)
    return o * lax.rsqrt(ms + NORM_EPS) * sg_ref[...] * (1.0 - li) * (z * jax.nn.sigmoid(z))


def _split_maps(qh):
    lane = lax.broadcasted_iota(jnp.int32, qh.shape, 1)
    zero = jnp.zeros_like(qh)
    return jnp.where(lane < DA_QK, qh, zero), jnp.where(lane >= DA_QK, qh, zero)


_QK_DIMS = (((1,), (1,)), ((), ()))


def _attn_prompt_body(q_ref, k_ref, vt_ref, z_ref, rel_ref, lam_ref, sg_ref, o_ref, bias_ref,
                      *, tq, layer):
    qi = pl.program_id(1)

    @pl.when((pl.program_id(0) == 0) & (qi == 0))
    def _():
        d = (lax.broadcasted_iota(jnp.int32, (tq, tq), 1)
             - lax.broadcasted_iota(jnp.int32, (tq, tq), 0))
        for blk in range(2):
            tiles = _bias_tiles(d + blk * tq, rel_ref)
            for h in range(DA_HEADS):
                bias_ref[blk * DA_HEADS + h] = tiles[h]

    lam, li = _lambda(lam_ref, layer)
    n_far = jnp.maximum(qi - 1, 0)
    heads = range(DA_HEADS)
    sls = [slice(h * DA_V, (h + 1) * DA_V) for h in heads]
    q_sts = [jnp.concatenate(_split_maps(q_ref[:, sl]), axis=0) for sl in sls]
    far_bias = [rel_ref[N_BUCKETS - 1, h] for h in heads]

    def attend(carries, kblk, nblk, biases):
        off = pl.multiple_of(kblk * tq, tq)
        vts = [jnp.concatenate([vt_ref[kblk + i, sl, :] for i in range(nblk)], axis=1)
               for sl in sls]
        return _softmax_update(carries, q_sts, [k_ref[pl.ds(off, nblk * tq), sl] for sl in sls],
                               vts, biases)

    init = (jnp.full((1, 2 * tq), NEG_INF, F32), jnp.zeros((1, 2 * tq), F32),
            jnp.zeros((DA_V, 2 * tq), F32))
    carries = lax.fori_loop(0, n_far >> 1, lambda j, c: attend(c, 2 * j, 2, far_bias),
                            (init,) * DA_HEADS)
    carries = lax.cond((n_far & 1) == 1, lambda c: attend(c, n_far - 1, 1, far_bias),
                       lambda c: c, carries)
    first = qi == 0
    near = []
    for h in heads:
        top = jnp.where(first, bias_ref[h], bias_ref[DA_HEADS + h])
        bottom = bias_ref[h] + jnp.where(first, NEG_INF, 0.0)
        tile = jnp.concatenate([top, bottom], axis=0)
        near.append(jnp.concatenate([tile, tile], axis=1))
    carries = attend(carries, n_far, 2, near)
    for h in heads:
        _, l, acc = carries[h]
        o_t = acc / l
        o_st = jnp.concatenate([o_t[:, :tq].T, o_t[:, tq:].T], axis=0)
        o_ref[:, sls[h]] = _head_out(o_st, z_ref[:, sls[h]], lam, li, sg_ref, tq)


def _attn_prompt(qn, kb, vt, pda, rel_bias, lam, sg, layer, tq):
    b, t, _ = qn.shape
    nblk = t // tq
    return pl.pallas_call(
        functools.partial(_attn_prompt_body, tq=tq, layer=layer),
        out_shape=jax.ShapeDtypeStruct((b, t, DA_WIDTH), F32),
        grid=(b, nblk),
        in_specs=[pl.BlockSpec((None, tq, DA_WIDTH), lambda i, q: (i, q, 0)),
                  pl.BlockSpec((None, t, DA_WIDTH), lambda i, q: (i, 0, 0)),
                  pl.BlockSpec((nblk, DA_WIDTH, tq), lambda i, q: (i, 0, 0)),
                  pl.BlockSpec((None, tq, DA_WIDTH), lambda i, q: (i, q, 3)),
                  pl.BlockSpec(memory_space=pltpu.SMEM),
                  pl.BlockSpec((4, DA_QK), lambda i, q: (0, 0)),
                  pl.BlockSpec((1, DA_V), lambda i, q: (0, 0))],
        out_specs=pl.BlockSpec((None, tq, DA_WIDTH), lambda i, q: (i, q, 0)),
        scratch_shapes=[pltpu.VMEM((2 * DA_HEADS, tq, tq), F32)],
        compiler_params=_cparams(("arbitrary", "arbitrary")),
        name="attn_prompt",
    )(qn, kb, vt, pda, rel_bias, lam, sg.reshape(1, DA_V))


def _attn_sample_body(pt_ref, q_ref, *refs, tq, npg, layer):
    kc = refs[:npg]
    vc = refs[npg:2 * npg]
    kn_ref, vn_ref, z_ref, rel_ref, lam_ref, sg_ref, o_ref, bias_ref = refs[2 * npg:]
    past = npg * PAGE_SIZE
    ncol = past + PAGE_SIZE

    @pl.when(pl.program_id(0) == 0)
    def _():
        t_idx = lax.rem(lax.broadcasted_iota(jnp.int32, (2 * tq, ncol), 0), tq)
        col = lax.broadcasted_iota(jnp.int32, (2 * tq, ncol), 1)
        tiles = _bias_tiles(past + t_idx - col, rel_ref)
        for h in range(DA_HEADS):
            bias_ref[h] = tiles[h]

    lam, li = _lambda(lam_ref, layer)
    pad = jnp.zeros((PAGE_SIZE - tq, DA_V), F32)
    outs = []
    for h in range(DA_HEADS):
        sl = slice(h * DA_V, (h + 1) * DA_V)
        q_st = jnp.concatenate(_split_maps(q_ref[:, sl]), axis=0)
        rows_h = pl.ds(h, PAGE_SIZE, stride=DA_HEADS)
        page_h = lambda ref: ref.reshape(PAGE_SIZE * DA_HEADS, DA_V)[rows_h, :]
        kh = jnp.concatenate([page_h(kc[j]) for j in range(npg)] + [kn_ref[:, sl], pad], axis=0)
        vh = jnp.concatenate([page_h(vc[j]) for j in range(npg)] + [vn_ref[:, sl], pad], axis=0)
        s = lax.dot_general(q_st, kh.astype(BF16), _QK_DIMS, preferred_element_type=F32)
        s = s + bias_ref[h]
        m = jnp.max(s, axis=1, keepdims=True)
        p = jnp.exp(s - m)
        l = jnp.sum(p, axis=1, keepdims=True)
        o = jnp.dot(p.astype(BF16), vh.astype(BF16), preferred_element_type=F32) / l
        outs.append(_head_out(o, z_ref[:, sl], lam, li, sg_ref, tq))
    o_ref[...] = jnp.concatenate(outs, axis=1)


def _attn_sample(qn, kn, pda, cache_k, cache_v, page_table, rel_bias, lam, sg, layer):
    b, tq, _ = qn.shape
    npg = page_table.shape[1]
    page_specs = [pl.BlockSpec((None, None, PAGE_SIZE, DA_HEADS, DA_V),
                               lambda i, pt, j=j: (layer, pt[i, j], 0, 0, 0)) for j in range(npg)]
    seq_spec = lambda col: pl.BlockSpec((None, tq, DA_WIDTH), lambda i, pt: (i, 0, col))
    grid_spec = pltpu.PrefetchScalarGridSpec(
        num_scalar_prefetch=1,
        grid=(b,),
        in_specs=[seq_spec(0)] + page_specs + page_specs
                 + [seq_spec(0), seq_spec(2), seq_spec(3),
                    pl.BlockSpec(memory_space=pltpu.SMEM),
                    pl.BlockSpec((4, DA_QK), lambda i, pt: (0, 0)),
                    pl.BlockSpec((1, DA_V), lambda i, pt: (0, 0))],
        out_specs=pl.BlockSpec((None, tq, DA_WIDTH), lambda i, pt: (i, 0, 0)),
        scratch_shapes=[pltpu.VMEM((DA_HEADS, 2 * tq, (npg + 1) * PAGE_SIZE), F32)],
    )
    return pl.pallas_call(
        functools.partial(_attn_sample_body, tq=tq, npg=npg, layer=layer),
        out_shape=jax.ShapeDtypeStruct((b, tq, DA_WIDTH), F32),
        grid_spec=grid_spec,
        compiler_params=_cparams(("arbitrary",)),
        name="attn_sample",
    )(page_table, qn, *([cache_k] * npg), *([cache_v] * npg), kn, pda, pda, rel_bias, lam,
      sg.reshape(1, DA_V))


def _out_proj_body(x_ref, y_ref, sz_ref, bz_ref, yb_ref, g_ref, lg_ref, lb_ref, e_ref,
                   wa_ref, wb_ref, wo_ref, o_ref):
    y = y_ref[...]
    mean = _head_sum(y, e_ref) * (1.0 / RW_HEAD)
    yc = y - mean
    var = _head_sum(yc * yc, e_ref) * (1.0 / RW_HEAD)
    yn = yc * lax.rsqrt(var + GN_EPS) * lg_ref[...] + lb_ref[...]
    ya = yn * sz_ref[...] + bz_ref[...]
    ga = jax.nn.sigmoid(g_ref[:, 0:D_MODEL])
    gb = jax.nn.sigmoid(g_ref[:, D_MODEL:2 * D_MODEL])
    m = (ga * jnp.dot(ya.astype(BF16), wa_ref[...], preferred_element_type=F32)
         + gb * jnp.dot(yb_ref[...].astype(BF16), wb_ref[...], preferred_element_type=F32))
    o_ref[...] = x_ref[...] + jnp.dot(m.astype(BF16), wo_ref[...], preferred_element_type=F32)


def _out_proj(x2d, y, sz, bz, yb, pg, lnx_g, lnx_b, e, wa, wb, wo, tm):
    n = x2d.shape[0]
    rows = lambda c: pl.BlockSpec((tm, c), lambda i: (i, 0))
    full = lambda r, c: pl.BlockSpec((r, c), lambda i: (0, 0))
    return pl.pallas_call(
        _out_proj_body,
        out_shape=jax.ShapeDtypeStruct((n, D_MODEL), F32),
        grid=(n // tm,),
        in_specs=[rows(D_MODEL), rows(RW_WIDTH), rows(RW_WIDTH), rows(RW_WIDTH), rows(DA_WIDTH),
                  rows(G_COLS), full(1, RW_WIDTH), full(1, RW_WIDTH), full(RW_WIDTH, RW_WIDTH),
                  full(RW_WIDTH, D_MODEL), full(DA_WIDTH, D_MODEL), full(D_MODEL, D_MODEL)],
        out_specs=rows(D_MODEL),
        compiler_params=_cparams(("arbitrary",)),
        name="out_proj",
    )(x2d, y, sz, bz, yb, pg, lnx_g.reshape(1, -1), lnx_b.reshape(1, -1), e, wa, wb, wo)


def _trunk(x, t_valid, tm, tt, shift0, state0, cache_k, cache_v, page_table, prm):
    b, t, _ = x.shape
    n = b * t
    e = (jnp.arange(RW_WIDTH)[:, None] // RW_HEAD == jnp.arange(RW_WIDTH)[None, :] // RW_HEAD)
    e = e.astype(BF16)
    new_k, new_v, new_s, new_shift = [], [], [], []
    for l in range(2):
        w_in = prm["w_in"][l].astype(BF16)
        w_rw = w_in[:, :RW_COLS]
        x2d = x.reshape(n, D_MODEL)
        p_rw = _in_proj(x2d, prm["norm_g"][l], w_rw, tm)
        pda = _in_proj(x2d, prm["norm_g"][l], w_in[:, RW_COLS:RW_COLS + DA_COLS], tm)
        pg = _in_proj(x2d, prm["norm_g"][l], w_in[:, RW_COLS + DA_COLS:], tm)
        new_shift.append(_norm_rows(x[:, t_valid - 1], prm["norm_g"][l]))

        if shift0 is None:
            prev = jnp.zeros((b, 1, RW_COLS), F32)
            s_init = jnp.zeros((b, RW_HEADS, RW_HEAD, RW_HEAD), F32)
        else:
            prev = _in_proj(shift0[l], prm["norm_g"][l], w_rw, shift0[l].shape[0],
                            normalize=False).reshape(b, 1, RW_COLS)
            s_init = state0[l]
        r, w, k, v, rem, rep, sz, bz = _rwkv_prep(
            p_rw.reshape(b, t, RW_COLS), prev, prm["mu_shift"][l], prm["w0"][l], prm["w2"][l],
            prm["a0"][l], prm["a2"][l], prm["k_k"][l], prm["k_a"][l], prm["r_k"][l], e, tt)
        y, s_fin = _rwkv_chunk_scan(r, w, k, v, rem, rep, s_init, t_valid)

        qn, kn, *kv = _da_prep(pda, prm["q_norm_g"][l], prm["k_norm_g"][l], e, tm, cache_k is None)
        lam = jnp.stack([prm["lam_q1"][l], prm["lam_k1"][l], prm["lam_q2"][l], prm["lam_k2"][l]])
        pda3 = pda.reshape(b, t, DA_COLS)
        qn3 = qn.reshape(b, t, DA_WIDTH)
        kn3 = kn.reshape(b, t, DA_WIDTH)
        if cache_k is None:
            yb = _attn_prompt(qn3, kv[0].reshape(b, t, DA_WIDTH), kv[1], pda3, prm["rel_bias"], lam,
                              prm["subln_g"][l], l, tt)
        else:
            yb = _attn_sample(qn3, kn3, pda3, cache_k, cache_v, page_table, prm["rel_bias"], lam,
                              prm["subln_g"][l], l)

        x = _out_proj(x2d, y.reshape(n, RW_WIDTH), sz.reshape(n, RW_WIDTH),
                      bz.reshape(n, RW_WIDTH), yb.reshape(n, DA_WIDTH), pg, prm["lnx_g"][l],
                      prm["lnx_b"][l], e, prm["w_a_out"][l].astype(BF16),
                      prm["w_b_out"][l].astype(BF16), prm["w_o"][l].astype(BF16),
                      tm).reshape(b, t, D_MODEL)
        new_k.append(kn3[:, :t_valid].reshape(b, t_valid, DA_HEADS, DA_V))
        new_v.append(pda3[:, :t_valid, 2 * DA_WIDTH:3 * DA_WIDTH].reshape(b, t_valid, DA_HEADS, DA_V))
        new_s.append(s_fin)
    return x, jnp.stack(new_k), jnp.stack(new_v), jnp.stack(new_s), jnp.stack(new_shift)


def kernel(x_prompt, x_sample, cache_k, cache_v, page_table, state_rwkv, state_shift, meta_tokens,
           rel_bias, norm_g, w_in, mu_shift, w0, w2, a0, a2, k_k, k_a, r_k, lnx_g, lnx_b,
           q_norm_g, k_norm_g, lam_q1, lam_k1, lam_q2, lam_k2, subln_g, w_a_out, w_b_out, w_o):
    prm = dict(rel_bias=rel_bias, norm_g=norm_g, w_in=w_in, mu_shift=mu_shift, w0=w0, w2=w2,
               a0=a0, a2=a2, k_k=k_k, k_a=k_a, r_k=r_k.reshape(2, RW_WIDTH), lnx_g=lnx_g,
               lnx_b=lnx_b, q_norm_g=q_norm_g, k_norm_g=k_norm_g, lam_q1=lam_q1, lam_k1=lam_k1,
               lam_q2=lam_q2, lam_k2=lam_k2, subln_g=subln_g, w_a_out=w_a_out, w_b_out=w_b_out,
               w_o=w_o)
    bp, seq, _ = x_prompt.shape
    t_valid = seq + N_META
    t_pad = -(-t_valid // SEQ_TILE) * SEQ_TILE
    meta = jnp.broadcast_to(meta_tokens[None], (bp, N_META, D_MODEL))
    xp = jnp.concatenate([meta, x_prompt, jnp.zeros((bp, t_pad - t_valid, D_MODEL), F32)], axis=1)
    yp, kp, vp, sp, hp = _trunk(xp, t_valid, 512, SEQ_TILE, None, None, None, None, None, prm)

    ts = x_sample.shape[1]
    ys, k_s, v_s, s_s, h_s = _trunk(x_sample, ts, 512, ts, state_shift, state_rwkv, cache_k,
                                    cache_v, page_table, prm)
    return (yp[:, N_META:t_valid], ys, kp, vp, sp, hp, k_s, v_s, s_s, h_s)
```

```python
import functools
import math

import jax
import jax.numpy as jnp
import numpy as np
from jax import lax
from jax.experimental import pallas as pl
from jax.experimental.pallas import tpu as pltpu

F32 = jnp.float32
BF16 = jnp.bfloat16

D_MODEL = 1024
N_META = 16
PAGE_SIZE = 128
RW_WIDTH = 512
RW_HEAD = 64
RW_HEADS = 8
LORA = 64
GN_EPS = 64e-5
DA_WIDTH = 512
DA_QK = 64
DA_HEADS = 4
DA_V = 128
N_BUCKETS = 32
MAX_DISTANCE = 128
NORM_EPS = 1e-6
NEG_INF = -1e30

RW_COLS = 3 * RW_WIDTH + 2 * LORA + RW_WIDTH
DA_COLS = 4 * DA_WIDTH
G_COLS = 2 * D_MODEL

SEQ_TILE = 128
VMEM_LIMIT = 48 * 1024 * 1024


def _cparams(sem):
    return pltpu.CompilerParams(dimension_semantics=sem, vmem_limit_bytes=VMEM_LIMIT)


def _head_sum(x, e_ref):
    hi = x.astype(BF16)
    lo = (x - hi.astype(F32)).astype(BF16)
    e = e_ref[...]
    return (jnp.dot(hi, e, preferred_element_type=F32)
            + jnp.dot(lo, e, preferred_element_type=F32))


def _in_proj_body(x_ref, g_ref, w_ref, o_ref, *, normalize):
    x = x_ref[...]
    if normalize:
        ms = jnp.mean(x * x, axis=-1, keepdims=True)
        x = x * lax.rsqrt(ms + NORM_EPS) * g_ref[...]
    o_ref[...] = jnp.dot(x.astype(BF16), w_ref[...], preferred_element_type=F32)


def _in_proj(x2d, g, w, tm, normalize=True):
    n, d = x2d.shape
    c = w.shape[1]
    return pl.pallas_call(
        functools.partial(_in_proj_body, normalize=normalize),
        out_shape=jax.ShapeDtypeStruct((n, c), F32),
        grid=(n // tm,),
        in_specs=[pl.BlockSpec((tm, d), lambda i: (i, 0)),
                  pl.BlockSpec((1, d), lambda i: (0, 0)),
                  pl.BlockSpec((d, c), lambda i: (0, 0))],
        out_specs=pl.BlockSpec((tm, c), lambda i: (i, 0)),
        compiler_params=_cparams(("arbitrary",)),
        name="in_proj",
    )(x2d, g.reshape(1, d), w)


def _norm_rows_body(x_ref, g_ref, o_ref):
    x = x_ref[...]
    ms = jnp.mean(x * x, axis=-1, keepdims=True)
    o_ref[...] = x * lax.rsqrt(ms + NORM_EPS) * g_ref[...]


def _norm_rows(x2d, g):
    n, d = x2d.shape
    return pl.pallas_call(
        _norm_rows_body,
        out_shape=jax.ShapeDtypeStruct((n, d), F32),
        name="norm_rows",
    )(x2d, g.reshape(1, d))


def _softplus(x):
    return jnp.maximum(x, 0.0) + jnp.log1p(jnp.exp(-jnp.abs(x)))


def _rwkv_prep(p, shifted, mu_ref, w0_ref, w2_ref, a0_ref, a2_ref, kk_ref, ka_ref, rk_ref, e_ref):
    h = p + (shifted - p) * mu_ref[...]
    r = h[:, 0:512]
    k = h[:, 512:1024]
    v = h[:, 1024:1536]
    wd = h[:, 1536:1600]
    ad = h[:, 1600:1664]
    z = h[:, 1664:2176]

    lw = w0_ref[...] + jnp.dot(jnp.tanh(wd).astype(BF16), w2_ref[...], preferred_element_type=F32)
    log_decay = -jnp.exp(-_softplus(-lw) - 0.5)
    rate = jax.nn.sigmoid(a0_ref[...] + jnp.dot(ad.astype(BF16), a2_ref[...],
                                                 preferred_element_type=F32))
    kk = k * kk_ref[...]
    ss = _head_sum(kk * kk, e_ref)
    kk = kk / jnp.maximum(jnp.sqrt(ss), 1e-12)
    kmod = k * (1.0 + (rate - 1.0) * ka_ref[...])
    bonus = _head_sum(r * kmod * rk_ref[...], e_ref) * v
    sz = z * jax.nn.sigmoid(z)
    return r, log_decay, kmod, v, -kk, kk * rate, sz, bonus * sz


PAIR = 2 * RW_HEAD
N_PAIRS = RW_HEADS // 2
SCAN_CHUNK = 64
SCAN_GROUPS = 4
INV_BASE = 8


def _mm(a, b):
    return jnp.dot(a.astype(BF16), b.astype(BF16), preferred_element_type=F32)


def _seg_cumsum(x, seg, reverse=False):
    n = x.shape[0]
    t = lax.rem(lax.broadcasted_iota(jnp.int32, x.shape, 0), seg)
    y = x
    s = 1
    while s < seg:
        if reverse:
            y = y + jnp.where(t < seg - s, pltpu.roll(y, n - s, axis=0), 0.0)
        else:
            y = y + jnp.where(t >= s, pltpu.roll(y, s, axis=0), 0.0)
        s *= 2
    return y - x if reverse else y


def _rwkv_mix_body(p_ref, prev_ref, s0_ref, mu_ref, w0_ref, w2_ref, a0_ref, a2_ref, kk_ref, ka_ref,
                   rk_ref, e_ref, y_ref, sz_ref, bz_ref, st_ref, h_ref, carry_ref,
                   *, ngrp, nseq, chunk, t_valid):
    c = pl.program_id(1)
    nblk = ngrp * nseq
    rows = nseq * chunk
    n2 = 2 * rows
    lane = lax.broadcasted_iota(jnp.int32, (rows, PAIR), 1)
    lo_half = lane < RW_HEAD

    @pl.when(c == 0)
    def _():
        carry_ref[...] = prev_ref[...]
        z = jnp.zeros((RW_HEAD, RW_HEAD), F32)
        for s in range(nblk):
            for p in range(N_PAIRS):
                blk = jnp.concatenate([jnp.concatenate([s0_ref[s, 2 * p], z], axis=1),
                                       jnp.concatenate([z, s0_ref[s, 2 * p + 1]], axis=1)], axis=0)
                h_ref[s, p] = blk.T

    p3 = p_ref[...]
    p_all = p3.reshape(nblk * chunk, RW_COLS)
    t_all = lax.rem(lax.broadcasted_iota(jnp.int32, p_all.shape, 0), chunk)
    carried = jnp.broadcast_to(carry_ref[...], p3.shape).reshape(p_all.shape)
    shifted = jnp.where(t_all == 0, carried, pltpu.roll(p_all, 1, axis=0))
    carry_ref[...] = p3[:, chunk - 1:chunk, :]
    r_all, lw_all, k_all, v_all, a_all, b_all, sz, bz = _rwkv_prep(
        p_all, shifted, mu_ref, w0_ref, w2_ref, a0_ref, a2_ref, kk_ref, ka_ref, rk_ref, e_ref)
    sz_ref[...] = sz.reshape(sz_ref.shape)
    bz_ref[...] = bz.reshape(bz_ref.shape)

    tpos = c * chunk + lax.rem(lax.broadcasted_iota(jnp.int32, (rows, RW_WIDTH), 0), chunk)
    valid = tpos < t_valid
    sls = [slice(p * PAIR, (p + 1) * PAIR) for p in range(N_PAIRS)]
    stack = lambda x: [jnp.concatenate([jnp.where(lo_half, x[:, sl], 0.0),
                                        jnp.where(lo_half, 0.0, x[:, sl])], axis=0) for sl in sls]
    a_st, r_st, b_st, k_st, bh_st, kh_st, v_st, e_pos = [], [], [], [], [], [], [], []
    for grp in range(ngrp):
        msk = lambda x: jnp.where(valid, x[grp * rows:(grp + 1) * rows], 0.0)
        r, lw, k, v, a, b = (msk(x) for x in (r_all, lw_all, k_all, v_all, a_all, b_all))
        g = _seg_cumsum(lw, chunk)
        e_neg = jnp.exp(-g)
        e_end = jnp.exp(_seg_cumsum(lw, chunk, reverse=True))
        e_pos.append(jnp.exp(g))
        a_st += stack(a * jnp.exp(g - lw))
        r_st += stack(r * e_pos[grp])
        b_st += stack(b * e_neg)
        k_st += stack(k * e_neg)
        bh_st += stack(b * e_end)
        kh_st += stack(k * e_end)
        v_st += stack(v)

    row2 = lax.broadcasted_iota(jnp.int32, (n2, n2), 0)
    col2 = lax.broadcasted_iota(jnp.int32, (n2, n2), 1)
    same_seq = (row2 // chunk) == (col2 // chunk)
    strict = same_seq & (row2 > col2)
    incl = same_seq & (row2 >= col2)
    zeros = jnp.zeros((n2, PAIR), F32)
    seq_of_row = lax.rem(lax.broadcasted_iota(jnp.int32, (n2, PAIR), 0), rows) // chunk
    seq_of_col = lax.rem(lax.broadcasted_iota(jnp.int32, (PAIR, 2 * n2), 1), rows) // chunk

    probs = range(ngrp * N_PAIRS)
    each = lambda f, *xs: [f(*(x[i] for x in xs)) for i in probs]
    gram = each(lambda ar, bk: lax.dot_general(ar.astype(BF16), bk.astype(BF16), _QK_DIMS,
                                               preferred_element_type=F32),
                each(lambda x, y: jnp.concatenate([x, y], axis=0), a_st, r_st),
                each(lambda x, y: jnp.concatenate([x, y], axis=0), b_st, k_st))
    mab = each(lambda g_: jnp.where(strict, g_[:n2, :n2], 0.0), gram)
    mak = each(lambda g_: jnp.where(strict, g_[:n2, n2:], 0.0), gram)
    mrbk = each(lambda g_: jnp.concatenate([jnp.where(incl, g_[n2:, :n2], 0.0),
                                            jnp.where(incl, g_[n2:, n2:], 0.0)], axis=1), gram)
    x1 = each(_mm, mak, v_st)

    same = lambda w: (row2 // w) == (col2 // w)
    md = each(lambda m: jnp.where(same(INV_BASE), m, 0.0), mab)
    m2 = each(_mm, md, md)
    m4 = each(_mm, m2, m2)
    t_inv = each(lambda m: (row2 == col2).astype(F32) + m, md)
    t_inv = each(lambda t, m: t + _mm(t, m), t_inv, m2)
    t_inv = each(lambda t, m: t + _mm(t, m), t_inv, m4)
    width = INV_BASE
    while width < chunk:
        off = each(lambda m: jnp.where(same(2 * width) & jnp.logical_not(same(width)), m, 0.0), mab)
        t_off = each(_mm, t_inv, off)
        t_inv = each(lambda t, to: t + _mm(to, t), t_inv, t_off)
        width *= 2

    tx = each(lambda t, x, a_: _mm(t, jnp.concatenate([x, a_], axis=1)), t_inv, x1, a_st)
    rh = each(lambda t, v_: jnp.concatenate(
        [jnp.concatenate([t[:, PAIR:], t[:, :PAIR]], axis=1),
         jnp.concatenate([zeros, v_], axis=1)], axis=0), tx, v_st)
    qy = each(_mm, mrbk, rh)
    q = each(lambda r_, qy_: r_ + qy_[:, :PAIR], r_st, qy)
    yst = each(lambda qy_: qy_[:, PAIR:], qy)
    bk_t = each(lambda x, y: jnp.concatenate([x, y], axis=0).T, bh_st, kh_st)
    only = nseq > 1
    for s in range(nseq):
        slot = lambda i: (i // N_PAIRS * nseq + s, i % N_PAIRS)
        h = [h_ref[slot(i)] for i in probs]
        yst = each(lambda y_, q_, h_: y_ + _mm(jnp.where(seq_of_row == s, q_, 0.0) if only else q_,
                                               h_), yst, q, h)
        gc = each(lambda b_, r_: _mm(jnp.where(seq_of_col == s, b_, 0.0) if only else b_, r_),
                  bk_t, rh)
        gh = each(lambda g_, h_: _mm(g_[:, :PAIR], h_), gc, h)
        last = s * chunk + chunk - 1
        for i in probs:
            g_end = e_pos[i // N_PAIRS][last:last + 1, sls[i % N_PAIRS]]
            g_col = jnp.broadcast_to(g_end, (PAIR, PAIR)).T
            h_ref[slot(i)] = g_col * h[i] + gh[i] + gc[i][:, PAIR:]
    for grp in range(ngrp):
        y_grp = [y_[:rows] + y_[rows:] for y_ in yst[grp * N_PAIRS:(grp + 1) * N_PAIRS]]
        y_ref[grp * nseq:(grp + 1) * nseq] = jnp.concatenate(y_grp, axis=1).reshape(
            nseq, chunk, RW_WIDTH)

    @pl.when(c == pl.num_programs(1) - 1)
    def _():
        for s in range(ngrp * nseq):
            for p in range(N_PAIRS):
                ht = h_ref[s, p].T
                st_ref[s, 2 * p] = ht[:RW_HEAD, :RW_HEAD]
                st_ref[s, 2 * p + 1] = ht[RW_HEAD:, RW_HEAD:]


def _rwkv_mix(p3, prev, s0, mu, w0, w2, a0, a2, k_k, k_a, r_k, e, t_valid):
    bsz, t, _ = p3.shape
    chunk = min(t, SCAN_CHUNK)
    nseq = SCAN_CHUNK // chunk
    ngrp = SCAN_GROUPS if nseq == 1 else 2
    nblk = ngrp * nseq
    row = lambda x: x.reshape(1, -1)
    seq = pl.BlockSpec((nblk, chunk, RW_WIDTH), lambda i, j: (i, j, 0))
    st = pl.BlockSpec((nblk, RW_HEADS, RW_HEAD, RW_HEAD), lambda i, j: (i, 0, 0, 0))
    full = lambda shape: pl.BlockSpec(shape, lambda i, j: (0,) * len(shape))
    out = jax.ShapeDtypeStruct((bsz, t, RW_WIDTH), F32)
    return pl.pallas_call(
        functools.partial(_rwkv_mix_body, ngrp=ngrp, nseq=nseq, chunk=chunk, t_valid=t_valid),
        out_shape=(out, out, out, jax.ShapeDtypeStruct((bsz, RW_HEADS, RW_HEAD, RW_HEAD), F32)),
        grid=(bsz // nblk, t // chunk),
        in_specs=[pl.BlockSpec((nblk, chunk, RW_COLS), lambda i, j: (i, j, 0)),
                  pl.BlockSpec((nblk, 1, RW_COLS), lambda i, j: (i, 0, 0)), st,
                  full((1, RW_COLS)), full((1, RW_WIDTH)), full((LORA, RW_WIDTH)),
                  full((1, RW_WIDTH)), full((LORA, RW_WIDTH)), full((1, RW_WIDTH)),
                  full((1, RW_WIDTH)), full((1, RW_WIDTH)), full((RW_WIDTH, RW_WIDTH))],
        out_specs=(seq, seq, seq, st),
        scratch_shapes=[pltpu.VMEM((nblk, N_PAIRS, PAIR, PAIR), F32),
                        pltpu.VMEM((nblk, 1, RW_COLS), F32)],
        compiler_params=_cparams(("arbitrary", "arbitrary")),
        name="rwkv_mix",
    )(p3, prev, s0, row(mu), row(w0), w2.astype(BF16), row(a0), a2.astype(BF16), row(k_k), row(k_a),
      row(r_k), e)


def _da_prep_body(p_ref, qg_ref, kg_ref, e_ref, q_ref, k_ref, *kv_refs, tm):
    q = p_ref[:, 0:512]
    k = p_ref[:, 512:1024]
    qm = _head_sum(q * q, e_ref) * (1.0 / DA_QK)
    km = _head_sum(k * k, e_ref) * (1.0 / DA_QK)
    qn = q * lax.rsqrt(qm + NORM_EPS) * qg_ref[...]
    kn = k * lax.rsqrt(km + NORM_EPS) * kg_ref[...]
    q_ref[...] = (qn * (DA_QK ** -0.5)).astype(BF16)
    k_ref[...] = kn
    if kv_refs:
        kb_ref, vt_ref = kv_refs
        kb_ref[...] = kn.astype(BF16)
        for j in range(tm // SEQ_TILE):
            vt_ref[j] = p_ref[j * SEQ_TILE:(j + 1) * SEQ_TILE, 1024:1536].T.astype(BF16)


def _da_prep(pda, qg, kg, e, tm, with_kv):
    n = pda.shape[0]
    tile8 = lambda g: jnp.tile(g, DA_WIDTH // DA_QK).reshape(1, DA_WIDTH)
    rows = pl.BlockSpec((tm, DA_WIDTH), lambda i: (i, 0))
    bf = jax.ShapeDtypeStruct((n, DA_WIDTH), BF16)
    out_shape = (bf, jax.ShapeDtypeStruct((n, DA_WIDTH), F32))
    out_specs = (rows, rows)
    if with_kv:
        out_shape += (bf, jax.ShapeDtypeStruct((n // SEQ_TILE, DA_WIDTH, SEQ_TILE), BF16))
        out_specs += (rows, pl.BlockSpec((tm // SEQ_TILE, DA_WIDTH, SEQ_TILE), lambda i: (i, 0, 0)))
    return pl.pallas_call(
        functools.partial(_da_prep_body, tm=tm),
        out_shape=out_shape,
        grid=(n // tm,),
        in_specs=[pl.BlockSpec((tm, DA_COLS), lambda i: (i, 0)),
                  pl.BlockSpec((1, DA_WIDTH), lambda i: (0, 0)),
                  pl.BlockSpec((1, DA_WIDTH), lambda i: (0, 0)),
                  pl.BlockSpec((RW_WIDTH, RW_WIDTH), lambda i: (0, 0))],
        out_specs=out_specs,
        compiler_params=_cparams(("arbitrary",)),
        name="da_prep",
    )(pda, tile8(qg), tile8(kg), e)


def _bucket_starts():
    max_exact = N_BUCKETS // 2
    n = np.arange(1, MAX_DISTANCE + 1)
    val = (np.log(n.astype(np.float32) / np.float32(max_exact))
           / np.float32(math.log(MAX_DISTANCE / max_exact)) * np.float32(N_BUCKETS - max_exact))
    inner = (n > max_exact) & (n < MAX_DISTANCE)
    assert np.all(np.abs(val[inner] - np.round(val[inner])) > 1e-3)
    bucket = np.where(n < max_exact, n, np.minimum(max_exact + val.astype(np.int32), N_BUCKETS - 1))
    return [int(n[np.argmax(bucket >= b)]) for b in range(1, N_BUCKETS)]


def _bias_tiles(n, rel_ref):
    accs = [jnp.full(n.shape, rel_ref[0, h], F32) for h in range(DA_HEADS)]
    for b, start in enumerate(_bucket_starts(), start=1):
        in_or_past = n >= start
        accs = [jnp.where(in_or_past, rel_ref[b, h], acc) for h, acc in enumerate(accs)]
    return [jnp.where(n >= 0, acc, NEG_INF) for acc in accs]


def _lambda(lam_ref, layer):
    lq1 = lam_ref[0:1, :]
    lk1 = lam_ref[1:2, :]
    lq2 = lam_ref[2:3, :]
    lk2 = lam_ref[3:4, :]
    li = 0.8 - 0.6 * math.exp(-0.3 * layer)
    lam = (jnp.exp(jnp.sum(lq1 * lk1, axis=1, keepdims=True))
           - jnp.exp(jnp.sum(lq2 * lk2, axis=1, keepdims=True)) + li)
    return lam, li


def _softmax_update(carries, q_sts, kbs, vts, biases):
    n = len(q_sts)
    score = lambda h: (lax.dot_general(kbs[h], q_sts[h], _QK_DIMS, preferred_element_type=F32)
                       + biases[h])
    out = []
    s_next = score(0)
    for h in range(n):
        s = s_next
        if h + 1 < n:
            s_next = score(h + 1)
        m, l, acc = carries[h]
        m_new = jnp.maximum(m, jnp.max(s, axis=0, keepdims=True))
        alpha = jnp.exp(m - m_new)
        p = jnp.exp(s - m_new)
        l = alpha * l + jnp.sum(p, axis=0, keepdims=True)
        acc = alpha * acc + jnp.dot(vts[h], p.astype(BF16), preferred_element_type=F32)
        out.append((m_new, l, acc))
    return tuple(out)


def _head_out(o_st, z, lam, li, sg_ref, tq):
    o = o_st[:tq] - lam * o_st[tq:]
    ms = jnp.mean(o * o, axis=1, keepdims=True)
    return o * lax.rsqrt(ms + NORM_EPS) * sg_ref[...] * (1.0 - li) * (z * jax.nn.sigmoid(z))


def _split_maps(qh):
    lane = lax.broadcasted_iota(jnp.int32, qh.shape, 1)
    zero = jnp.zeros_like(qh)
    return jnp.where(lane < DA_QK, qh, zero), jnp.where(lane >= DA_QK, qh, zero)


_QK_DIMS = (((1,), (1,)), ((), ()))


def _attn_prompt_body(q_ref, k_ref, vt_ref, z_ref, rel_ref, lam_ref, sg_ref, o_ref, bias_ref,
                      *, tq, layer):
    qi = pl.program_id(1)

    @pl.when((pl.program_id(0) == 0) & (qi == 0))
    def _():
        d = (lax.broadcasted_iota(jnp.int32, (tq, tq), 1)
             - lax.broadcasted_iota(jnp.int32, (tq, tq), 0))
        for blk in range(2):
            tiles = _bias_tiles(d + blk * tq, rel_ref)
            for h in range(DA_HEADS):
                bias_ref[blk * DA_HEADS + h] = tiles[h]

    lam, li = _lambda(lam_ref, layer)
    n_far = jnp.maximum(qi - 1, 0)
    heads = range(DA_HEADS)
    sls = [slice(h * DA_V, (h + 1) * DA_V) for h in heads]
    q_sts = [jnp.concatenate(_split_maps(q_ref[:, sl]), axis=0) for sl in sls]
    far_bias = [rel_ref[N_BUCKETS - 1, h] for h in heads]

    def attend(carries, kblk, nblk, biases):
        off = pl.multiple_of(kblk * tq, tq)
        vts = [jnp.concatenate([vt_ref[kblk + i, sl, :] for i in range(nblk)], axis=1)
               for sl in sls]
        return _softmax_update(carries, q_sts, [k_ref[pl.ds(off, nblk * tq), sl] for sl in sls],
                               vts, biases)

    init = (jnp.full((1, 2 * tq), NEG_INF, F32), jnp.zeros((1, 2 * tq), F32),
            jnp.zeros((DA_V, 2 * tq), F32))
    carries = lax.fori_loop(0, n_far >> 1, lambda j, c: attend(c, 2 * j, 2, far_bias),
                            (init,) * DA_HEADS)
    carries = lax.cond((n_far & 1) == 1, lambda c: attend(c, n_far - 1, 1, far_bias),
                       lambda c: c, carries)
    first = qi == 0
    near = []
    for h in heads:
        top = jnp.where(first, bias_ref[h], bias_ref[DA_HEADS + h])
        bottom = bias_ref[h] + jnp.where(first, NEG_INF, 0.0)
        tile = jnp.concatenate([top, bottom], axis=0)
        near.append(jnp.concatenate([tile, tile], axis=1))
    carries = attend(carries, n_far, 2, near)
    for h in heads:
        _, l, acc = carries[h]
        o_t = acc / l
        o_st = jnp.concatenate([o_t[:, :tq].T, o_t[:, tq:].T], axis=0)
        o_ref[:, sls[h]] = _head_out(o_st, z_ref[:, sls[h]], lam, li, sg_ref, tq)


def _attn_prompt(qn, kb, vt, pda, rel_bias, lam, sg, layer, tq):
    b, t, _ = qn.shape
    nblk = t // tq
    return pl.pallas_call(
        functools.partial(_attn_prompt_body, tq=tq, layer=layer),
        out_shape=jax.ShapeDtypeStruct((b, t, DA_WIDTH), F32),
        grid=(b, nblk),
        in_specs=[pl.BlockSpec((None, tq, DA_WIDTH), lambda i, q: (i, q, 0)),
                  pl.BlockSpec((None, t, DA_WIDTH), lambda i, q: (i, 0, 0)),
                  pl.BlockSpec((nblk, DA_WIDTH, tq), lambda i, q: (i, 0, 0)),
                  pl.BlockSpec((None, tq, DA_WIDTH), lambda i, q: (i, q, 3)),
                  pl.BlockSpec(memory_space=pltpu.SMEM),
                  pl.BlockSpec((4, DA_QK), lambda i, q: (0, 0)),
                  pl.BlockSpec((1, DA_V), lambda i, q: (0, 0))],
        out_specs=pl.BlockSpec((None, tq, DA_WIDTH), lambda i, q: (i, q, 0)),
        scratch_shapes=[pltpu.VMEM((2 * DA_HEADS, tq, tq), F32)],
        compiler_params=_cparams(("arbitrary", "arbitrary")),
        name="attn_prompt",
    )(qn, kb, vt, pda, rel_bias, lam, sg.reshape(1, DA_V))


def _attn_sample_body(pt_ref, q_ref, *refs, tq, npg, layer):
    kc = refs[:npg]
    vc = refs[npg:2 * npg]
    kn_ref, vn_ref, z_ref, rel_ref, lam_ref, sg_ref, o_ref, bias_ref = refs[2 * npg:]
    past = npg * PAGE_SIZE
    ncol = past + PAGE_SIZE

    @pl.when(pl.program_id(0) == 0)
    def _():
        t_idx = lax.rem(lax.broadcasted_iota(jnp.int32, (2 * tq, ncol), 0), tq)
        col = lax.broadcasted_iota(jnp.int32, (2 * tq, ncol), 1)
        tiles = _bias_tiles(past + t_idx - col, rel_ref)
        for h in range(DA_HEADS):
            bias_ref[h] = tiles[h]

    lam, li = _lambda(lam_ref, layer)
    pad = jnp.zeros((PAGE_SIZE - tq, DA_V), F32)
    outs = []
    for h in range(DA_HEADS):
        sl = slice(h * DA_V, (h + 1) * DA_V)
        q_st = jnp.concatenate(_split_maps(q_ref[:, sl]), axis=0)
        rows_h = pl.ds(h, PAGE_SIZE, stride=DA_HEADS)
        page_h = lambda ref: ref.reshape(PAGE_SIZE * DA_HEADS, DA_V)[rows_h, :]
        kh = jnp.concatenate([page_h(kc[j]) for j in range(npg)] + [kn_ref[:, sl], pad], axis=0)
        vh = jnp.concatenate([page_h(vc[j]) for j in range(npg)] + [vn_ref[:, sl], pad], axis=0)
        s = lax.dot_general(q_st, kh.astype(BF16), _QK_DIMS, preferred_element_type=F32)
        s = s + bias_ref[h]
        m = jnp.max(s, axis=1, keepdims=True)
        p = jnp.exp(s - m)
        l = jnp.sum(p, axis=1, keepdims=True)
        o = jnp.dot(p.astype(BF16), vh.astype(BF16), preferred_element_type=F32) / l
        outs.append(_head_out(o, z_ref[:, sl], lam, li, sg_ref, tq))
    o_ref[...] = jnp.concatenate(outs, axis=1)


def _attn_sample(qn, kn, pda, cache_k, cache_v, page_table, rel_bias, lam, sg, layer):
    b, tq, _ = qn.shape
    npg = page_table.shape[1]
    page_specs = [pl.BlockSpec((None, None, PAGE_SIZE, DA_HEADS, DA_V),
                               lambda i, pt, j=j: (layer, pt[i, j], 0, 0, 0)) for j in range(npg)]
    seq_spec = lambda col: pl.BlockSpec((None, tq, DA_WIDTH), lambda i, pt: (i, 0, col))
    grid_spec = pltpu.PrefetchScalarGridSpec(
        num_scalar_prefetch=1,
        grid=(b,),
        in_specs=[seq_spec(0)] + page_specs + page_specs
                 + [seq_spec(0), seq_spec(2), seq_spec(3),
                    pl.BlockSpec(memory_space=pltpu.SMEM),
                    pl.BlockSpec((4, DA_QK), lambda i, pt: (0, 0)),
                    pl.BlockSpec((1, DA_V), lambda i, pt: (0, 0))],
        out_specs=pl.BlockSpec((None, tq, DA_WIDTH), lambda i, pt: (i, 0, 0)),
        scratch_shapes=[pltpu.VMEM((DA_HEADS, 2 * tq, (npg + 1) * PAGE_SIZE), F32)],
    )
    return pl.pallas_call(
        functools.partial(_attn_sample_body, tq=tq, npg=npg, layer=layer),
        out_shape=jax.ShapeDtypeStruct((b, tq, DA_WIDTH), F32),
        grid_spec=grid_spec,
        compiler_params=_cparams(("arbitrary",)),
        name="attn_sample",
    )(page_table, qn, *([cache_k] * npg), *([cache_v] * npg), kn, pda, pda, rel_bias, lam,
      sg.reshape(1, DA_V))


def _out_proj_body(x_ref, y_ref, sz_ref, bz_ref, yb_ref, g_ref, lg_ref, lb_ref, e_ref,
                   wa_ref, wb_ref, wo_ref, o_ref):
    y = y_ref[...]
    mean = _head_sum(y, e_ref) * (1.0 / RW_HEAD)
    yc = y - mean
    var = _head_sum(yc * yc, e_ref) * (1.0 / RW_HEAD)
    yn = yc * lax.rsqrt(var + GN_EPS) * lg_ref[...] + lb_ref[...]
    ya = yn * sz_ref[...] + bz_ref[...]
    ga = jax.nn.sigmoid(g_ref[:, 0:D_MODEL])
    gb = jax.nn.sigmoid(g_ref[:, D_MODEL:2 * D_MODEL])
    m = (ga * jnp.dot(ya.astype(BF16), wa_ref[...], preferred_element_type=F32)
         + gb * jnp.dot(yb_ref[...].astype(BF16), wb_ref[...], preferred_element_type=F32))
    o_ref[...] = x_ref[...] + jnp.dot(m.astype(BF16), wo_ref[...], preferred_element_type=F32)


def _out_proj(x2d, y, sz, bz, yb, pg, lnx_g, lnx_b, e, wa, wb, wo, tm):
    n = x2d.shape[0]
    rows = lambda c: pl.BlockSpec((tm, c), lambda i: (i, 0))
    full = lambda r, c: pl.BlockSpec((r, c), lambda i: (0, 0))
    return pl.pallas_call(
        _out_proj_body,
        out_shape=jax.ShapeDtypeStruct((n, D_MODEL), F32),
        grid=(n // tm,),
        in_specs=[rows(D_MODEL), rows(RW_WIDTH), rows(RW_WIDTH), rows(RW_WIDTH), rows(DA_WIDTH),
                  rows(G_COLS), full(1, RW_WIDTH), full(1, RW_WIDTH), full(RW_WIDTH, RW_WIDTH),
                  full(RW_WIDTH, D_MODEL), full(DA_WIDTH, D_MODEL), full(D_MODEL, D_MODEL)],
        out_specs=rows(D_MODEL),
        compiler_params=_cparams(("arbitrary",)),
        name="out_proj",
    )(x2d, y, sz, bz, yb, pg, lnx_g.reshape(1, -1), lnx_b.reshape(1, -1), e, wa, wb, wo)


def _trunk(x, t_valid, tm, tt, shift0, state0, cache_k, cache_v, page_table, prm):
    b, t, _ = x.shape
    n = b * t
    e = (jnp.arange(RW_WIDTH)[:, None] // RW_HEAD == jnp.arange(RW_WIDTH)[None, :] // RW_HEAD)
    e = e.astype(BF16)
    new_k, new_v, new_s, new_shift = [], [], [], []
    for l in range(2):
        w_in = prm["w_in"][l]
        w_rw = w_in[:, :RW_COLS].astype(BF16)
        x2d = x.reshape(n, D_MODEL)
        p_rw = _in_proj(x2d, prm["norm_g"][l], w_rw, tm)
        pda = _in_proj(x2d, prm["norm_g"][l], w_in[:, RW_COLS:RW_COLS + DA_COLS].astype(BF16), tm)
        pg = _in_proj(x2d, prm["norm_g"][l], w_in[:, RW_COLS + DA_COLS:].astype(BF16), tm)
        new_shift.append(_norm_rows(x[:, t_valid - 1], prm["norm_g"][l]))

        if shift0 is None:
            prev = jnp.zeros((b, 1, RW_COLS), F32)
            s_init = jnp.zeros((b, RW_HEADS, RW_HEAD, RW_HEAD), F32)
        else:
            prev = _in_proj(shift0[l], prm["norm_g"][l], w_rw, shift0[l].shape[0],
                            normalize=False).reshape(b, 1, RW_COLS)
            s_init = state0[l]
        y, sz, bz, s_fin = _rwkv_mix(
            p_rw.reshape(b, t, RW_COLS), prev, s_init, prm["mu_shift"][l], prm["w0"][l],
            prm["w2"][l], prm["a0"][l], prm["a2"][l], prm["k_k"][l], prm["k_a"][l], prm["r_k"][l],
            e, t_valid)

        qn, kn, *kv = _da_prep(pda, prm["q_norm_g"][l], prm["k_norm_g"][l], e, tm, cache_k is None)
        lam = jnp.stack([prm["lam_q1"][l], prm["lam_k1"][l], prm["lam_q2"][l], prm["lam_k2"][l]])
        pda3 = pda.reshape(b, t, DA_COLS)
        qn3 = qn.reshape(b, t, DA_WIDTH)
        kn3 = kn.reshape(b, t, DA_WIDTH)
        if cache_k is None:
            yb = _attn_prompt(qn3, kv[0].reshape(b, t, DA_WIDTH), kv[1], pda3, prm["rel_bias"], lam,
                              prm["subln_g"][l], l, tt)
        else:
            yb = _attn_sample(qn3, kn3, pda3, cache_k, cache_v, page_table, prm["rel_bias"], lam,
                              prm["subln_g"][l], l)

        x = _out_proj(x2d, y.reshape(n, RW_WIDTH), sz.reshape(n, RW_WIDTH),
                      bz.reshape(n, RW_WIDTH), yb.reshape(n, DA_WIDTH), pg, prm["lnx_g"][l],
                      prm["lnx_b"][l], e, prm["w_a_out"][l].astype(BF16),
                      prm["w_b_out"][l].astype(BF16), prm["w_o"][l].astype(BF16),
                      tm).reshape(b, t, D_MODEL)
        new_k.append(kn3[:, :t_valid].reshape(b, t_valid, DA_HEADS, DA_V))
        new_v.append(pda3[:, :t_valid, 2 * DA_WIDTH:3 * DA_WIDTH].reshape(b, t_valid, DA_HEADS, DA_V))
        new_s.append(s_fin)
    return x, jnp.stack(new_k), jnp.stack(new_v), jnp.stack(new_s), jnp.stack(new_shift)


def kernel(x_prompt, x_sample, cache_k, cache_v, page_table, state_rwkv, state_shift, meta_tokens,
           rel_bias, norm_g, w_in, mu_shift, w0, w2, a0, a2, k_k, k_a, r_k, lnx_g, lnx_b,
           q_norm_g, k_norm_g, lam_q1, lam_k1, lam_q2, lam_k2, subln_g, w_a_out, w_b_out, w_o):
    prm = dict(rel_bias=rel_bias, norm_g=norm_g, w_in=w_in, mu_shift=mu_shift, w0=w0, w2=w2,
               a0=a0, a2=a2, k_k=k_k, k_a=k_a, r_k=r_k.reshape(2, RW_WIDTH), lnx_g=lnx_g,
               lnx_b=lnx_b, q_norm_g=q_norm_g, k_norm_g=k_norm_g, lam_q1=lam_q1, lam_k1=lam_k1,
               lam_q2=lam_q2, lam_k2=lam_k2, subln_g=subln_g, w_a_out=w_a_out, w_b_out=w_b_out,
               w_o=w_o)
    bp, seq, _ = x_prompt.shape
    t_valid = seq + N_META
    t_pad = -(-t_valid // SEQ_TILE) * SEQ_TILE
    meta = jnp.broadcast_to(meta_tokens[None], (bp, N_META, D_MODEL))
    xp = jnp.concatenate([meta, x_prompt, jnp.zeros((bp, t_pad - t_valid, D_MODEL), F32)], axis=1)
    yp, kp, vp, sp, hp = _trunk(xp, t_valid, 512, SEQ_TILE, None, None, None, None, None, prm)

    ts = x_sample.shape[1]
    ys, k_s, v_s, s_s, h_s = _trunk(x_sample, ts, 512, ts, state_shift, state_rwkv, cache_k,
                                    cache_v, page_table, prm)
    return (yp[:, N_META:t_valid], ys, kp, vp, sp, hp, k_s, v_s, s_s, h_s)
```

```python
import functools
import math

import jax
import jax.numpy as jnp
import numpy as np
from jax import lax
from jax.experimental import pallas as pl
from jax.experimental.pallas import tpu as pltpu

F32 = jnp.float32
BF16 = jnp.bfloat16

D_MODEL = 1024
N_META = 16
PAGE_SIZE = 128
RW_WIDTH = 512
RW_HEAD = 64
RW_HEADS = 8
LORA = 64
GN_EPS = 64e-5
DA_WIDTH = 512
DA_QK = 64
DA_HEADS = 4
DA_V = 128
N_BUCKETS = 32
MAX_DISTANCE = 128
NORM_EPS = 1e-6
NEG_INF = -1e30

RW_COLS = 3 * RW_WIDTH + 2 * LORA + RW_WIDTH
DA_COLS = 4 * DA_WIDTH
G_COLS = 2 * D_MODEL

SEQ_TILE = 128
VMEM_LIMIT = 48 * 1024 * 1024


def _cparams(sem):
    return pltpu.CompilerParams(dimension_semantics=sem, vmem_limit_bytes=VMEM_LIMIT)


def _head_sum(x, e_ref):
    hi = x.astype(BF16)
    lo = (x - hi.astype(F32)).astype(BF16)
    e = e_ref[...]
    return (jnp.dot(hi, e, preferred_element_type=F32)
            + jnp.dot(lo, e, preferred_element_type=F32))


def _in_proj_body(x_ref, g_ref, w_ref, o_ref, *, normalize):
    x = x_ref[...]
    if normalize:
        ms = jnp.mean(x * x, axis=-1, keepdims=True)
        x = x * lax.rsqrt(ms + NORM_EPS) * g_ref[...]
    o_ref[...] = jnp.dot(x.astype(BF16), w_ref[...], preferred_element_type=F32)


def _in_proj(x2d, g, w, tm, normalize=True):
    n, d = x2d.shape
    c = w.shape[1]
    return pl.pallas_call(
        functools.partial(_in_proj_body, normalize=normalize),
        out_shape=jax.ShapeDtypeStruct((n, c), F32),
        grid=(n // tm,),
        in_specs=[pl.BlockSpec((tm, d), lambda i: (i, 0)),
                  pl.BlockSpec((1, d), lambda i: (0, 0)),
                  pl.BlockSpec((d, c), lambda i: (0, 0))],
        out_specs=pl.BlockSpec((tm, c), lambda i: (i, 0)),
        compiler_params=_cparams(("arbitrary",)),
        name="in_proj",
    )(x2d, g.reshape(1, d), w)


def _norm_rows_body(x_ref, g_ref, o_ref):
    x = x_ref[...]
    ms = jnp.mean(x * x, axis=-1, keepdims=True)
    o_ref[...] = x * lax.rsqrt(ms + NORM_EPS) * g_ref[...]


def _norm_rows(x2d, g):
    n, d = x2d.shape
    return pl.pallas_call(
        _norm_rows_body,
        out_shape=jax.ShapeDtypeStruct((n, d), F32),
        name="norm_rows",
    )(x2d, g.reshape(1, d))


def _softplus(x):
    return jnp.maximum(x, 0.0) + jnp.log1p(jnp.exp(-jnp.abs(x)))


def _rwkv_prep(p, shifted, mu_ref, w0_ref, w2_ref, a0_ref, a2_ref, kk_ref, ka_ref, rk_ref, e_ref):
    h = p + (shifted - p) * mu_ref[...]
    r = h[:, 0:512]
    k = h[:, 512:1024]
    v = h[:, 1024:1536]
    wd = h[:, 1536:1600]
    ad = h[:, 1600:1664]
    z = h[:, 1664:2176]

    lw = w0_ref[...] + jnp.dot(jnp.tanh(wd).astype(BF16), w2_ref[...], preferred_element_type=F32)
    log_decay = -jnp.exp(-_softplus(-lw) - 0.5)
    rate = jax.nn.sigmoid(a0_ref[...] + jnp.dot(ad.astype(BF16), a2_ref[...],
                                                 preferred_element_type=F32))
    kk = k * kk_ref[...]
    ss = _head_sum(kk * kk, e_ref)
    kk = kk / jnp.maximum(jnp.sqrt(ss), 1e-12)
    kmod = k * (1.0 + (rate - 1.0) * ka_ref[...])
    bonus = _head_sum(r * kmod * rk_ref[...], e_ref) * v
    sz = z * jax.nn.sigmoid(z)
    return r, log_decay, kmod, v, -kk, kk * rate, sz, bonus * sz


PAIR = 2 * RW_HEAD
N_PAIRS = RW_HEADS // 2
SCAN_CHUNK = 64
SCAN_GROUPS = 4
INV_BASE = 8


def _mm(a, b):
    return jnp.dot(a.astype(BF16), b.astype(BF16), preferred_element_type=F32)


def _seg_cumsum(x, seg, reverse=False):
    n = x.shape[0]
    t = lax.rem(lax.broadcasted_iota(jnp.int32, x.shape, 0), seg)
    y = x
    s = 1
    while s < seg:
        if reverse:
            y = y + jnp.where(t < seg - s, pltpu.roll(y, n - s, axis=0), 0.0)
        else:
            y = y + jnp.where(t >= s, pltpu.roll(y, s, axis=0), 0.0)
        s *= 2
    return y - x if reverse else y


def _rwkv_mix_body(p_ref, prev_ref, s0_ref, mu_ref, w0_ref, w2_ref, a0_ref, a2_ref, kk_ref, ka_ref,
                   rk_ref, e_ref, y_ref, sz_ref, bz_ref, st_ref, h_ref, carry_ref,
                   *, ngrp, nseq, chunk, t_valid):
    c = pl.program_id(1)
    nblk = ngrp * nseq
    rows = nseq * chunk
    n2 = 2 * rows
    lane = lax.broadcasted_iota(jnp.int32, (rows, PAIR), 1)
    lo_half = lane < RW_HEAD

    @pl.when(c == 0)
    def _():
        carry_ref[...] = prev_ref[...]
        z = jnp.zeros((RW_HEAD, RW_HEAD), F32)
        for s in range(nblk):
            for p in range(N_PAIRS):
                blk = jnp.concatenate([jnp.concatenate([s0_ref[s, 2 * p], z], axis=1),
                                       jnp.concatenate([z, s0_ref[s, 2 * p + 1]], axis=1)], axis=0)
                h_ref[s, p] = blk.T

    p3 = p_ref[...]
    p_all = p3.reshape(nblk * chunk, RW_COLS)
    t_all = lax.rem(lax.broadcasted_iota(jnp.int32, p_all.shape, 0), chunk)
    carried = jnp.broadcast_to(carry_ref[...], p3.shape).reshape(p_all.shape)
    shifted = jnp.where(t_all == 0, carried, pltpu.roll(p_all, 1, axis=0))
    carry_ref[...] = p3[:, chunk - 1:chunk, :]
    r_all, lw_all, k_all, v_all, a_all, b_all, sz, bz = _rwkv_prep(
        p_all, shifted, mu_ref, w0_ref, w2_ref, a0_ref, a2_ref, kk_ref, ka_ref, rk_ref, e_ref)
    sz_ref[...] = sz.reshape(sz_ref.shape)
    bz_ref[...] = bz.reshape(bz_ref.shape)

    tpos = c * chunk + lax.rem(lax.broadcasted_iota(jnp.int32, (rows, RW_WIDTH), 0), chunk)
    valid = tpos < t_valid
    sls = [slice(p * PAIR, (p + 1) * PAIR) for p in range(N_PAIRS)]
    stack = lambda x: [jnp.concatenate([jnp.where(lo_half, x[:, sl], 0.0),
                                        jnp.where(lo_half, 0.0, x[:, sl])], axis=0) for sl in sls]
    a_st, r_st, b_st, k_st, bh_st, kh_st, v_st, e_pos = [], [], [], [], [], [], [], []
    for grp in range(ngrp):
        msk = lambda x: jnp.where(valid, x[grp * rows:(grp + 1) * rows], 0.0)
        r, lw, k, v, a, b = (msk(x) for x in (r_all, lw_all, k_all, v_all, a_all, b_all))
        g = _seg_cumsum(lw, chunk)
        e_neg = jnp.exp(-g)
        e_end = jnp.exp(_seg_cumsum(lw, chunk, reverse=True))
        e_pos.append(jnp.exp(g))
        a_st += stack(a * jnp.exp(g - lw))
        r_st += stack(r * e_pos[grp])
        b_st += stack(b * e_neg)
        k_st += stack(k * e_neg)
        bh_st += stack(b * e_end)
        kh_st += stack(k * e_end)
        v_st += stack(v)

    row2 = lax.broadcasted_iota(jnp.int32, (n2, n2), 0)
    col2 = lax.broadcasted_iota(jnp.int32, (n2, n2), 1)
    same_seq = (row2 // chunk) == (col2 // chunk)
    strict = same_seq & (row2 > col2)
    incl = same_seq & (row2 >= col2)
    zeros = jnp.zeros((n2, PAIR), F32)
    seq_of_row = lax.rem(lax.broadcasted_iota(jnp.int32, (n2, PAIR), 0), rows) // chunk
    seq_of_col = lax.rem(lax.broadcasted_iota(jnp.int32, (PAIR, 2 * n2), 1), rows) // chunk

    probs = range(ngrp * N_PAIRS)
    each = lambda f, *xs: [f(*(x[i] for x in xs)) for i in probs]
    gram = each(lambda ar, bk: lax.dot_general(ar.astype(BF16), bk.astype(BF16), _QK_DIMS,
                                               preferred_element_type=F32),
                each(lambda x, y: jnp.concatenate([x, y], axis=0), a_st, r_st),
                each(lambda x, y: jnp.concatenate([x, y], axis=0), b_st, k_st))
    mab = each(lambda g_: jnp.where(strict, g_[:n2, :n2], 0.0), gram)
    mak = each(lambda g_: jnp.where(strict, g_[:n2, n2:], 0.0), gram)
    mrbk = each(lambda g_: jnp.concatenate([jnp.where(incl, g_[n2:, :n2], 0.0),
                                            jnp.where(incl, g_[n2:, n2:], 0.0)], axis=1), gram)
    x1 = each(_mm, mak, v_st)

    same = lambda w: (row2 // w) == (col2 // w)
    md = each(lambda m: jnp.where(same(INV_BASE), m, 0.0), mab)
    m2 = each(_mm, md, md)
    m4 = each(_mm, m2, m2)
    t_inv = each(lambda m: (row2 == col2).astype(F32) + m, md)
    t_inv = each(lambda t, m: t + _mm(t, m), t_inv, m2)
    t_inv = each(lambda t, m: t + _mm(t, m), t_inv, m4)
    width = INV_BASE
    while width < chunk:
        off = each(lambda m: jnp.where(same(2 * width) & jnp.logical_not(same(width)), m, 0.0), mab)
        t_off = each(_mm, t_inv, off)
        t_inv = each(lambda t, to: t + _mm(to, t), t_inv, t_off)
        width *= 2

    tx = each(lambda t, x, a_: _mm(t, jnp.concatenate([x, a_], axis=1)), t_inv, x1, a_st)
    rh = each(lambda t, v_: jnp.concatenate(
        [jnp.concatenate([t[:, PAIR:], t[:, :PAIR]], axis=1),
         jnp.concatenate([zeros, v_], axis=1)], axis=0), tx, v_st)
    qy = each(_mm, mrbk, rh)
    q = each(lambda r_, qy_: r_ + qy_[:, :PAIR], r_st, qy)
    yst = each(lambda qy_: qy_[:, PAIR:], qy)
    bk_t = each(lambda x, y: jnp.concatenate([x, y], axis=0).T, bh_st, kh_st)
    only = nseq > 1
    for s in range(nseq):
        slot = lambda i: (i // N_PAIRS * nseq + s, i % N_PAIRS)
        h = [h_ref[slot(i)] for i in probs]
        yst = each(lambda y_, q_, h_: y_ + _mm(jnp.where(seq_of_row == s, q_, 0.0) if only else q_,
                                               h_), yst, q, h)
        gc = each(lambda b_, r_: _mm(jnp.where(seq_of_col == s, b_, 0.0) if only else b_, r_),
                  bk_t, rh)
        gh = each(lambda g_, h_: _mm(g_[:, :PAIR], h_), gc, h)
        last = s * chunk + chunk - 1
        for i in probs:
            g_end = e_pos[i // N_PAIRS][last:last + 1, sls[i % N_PAIRS]]
            g_col = jnp.broadcast_to(g_end, (PAIR, PAIR)).T
            h_ref[slot(i)] = g_col * h[i] + gh[i] + gc[i][:, PAIR:]
    for grp in range(ngrp):
        y_grp = [y_[:rows] + y_[rows:] for y_ in yst[grp * N_PAIRS:(grp + 1) * N_PAIRS]]
        y_ref[grp * nseq:(grp + 1) * nseq] = jnp.concatenate(y_grp, axis=1).reshape(
            nseq, chunk, RW_WIDTH)

    @pl.when(c == pl.num_programs(1) - 1)
    def _():
        for s in range(ngrp * nseq):
            for p in range(N_PAIRS):
                ht = h_ref[s, p].T
                st_ref[s, 2 * p] = ht[:RW_HEAD, :RW_HEAD]
                st_ref[s, 2 * p + 1] = ht[RW_HEAD:, RW_HEAD:]


def _rwkv_mix(p3, prev, s0, mu, w0, w2, a0, a2, k_k, k_a, r_k, e, t_valid):
    bsz, t, _ = p3.shape
    chunk = min(t, SCAN_CHUNK)
    nseq = SCAN_CHUNK // chunk
    ngrp = SCAN_GROUPS if nseq == 1 else 2
    nblk = ngrp * nseq
    row = lambda x: x.reshape(1, -1)
    seq = pl.BlockSpec((nblk, chunk, RW_WIDTH), lambda i, j: (i, j, 0))
    st = pl.BlockSpec((nblk, RW_HEADS, RW_HEAD, RW_HEAD), lambda i, j: (i, 0, 0, 0))
    full = lambda shape: pl.BlockSpec(shape, lambda i, j: (0,) * len(shape))
    out = jax.ShapeDtypeStruct((bsz, t, RW_WIDTH), F32)
    return pl.pallas_call(
        functools.partial(_rwkv_mix_body, ngrp=ngrp, nseq=nseq, chunk=chunk, t_valid=t_valid),
        out_shape=(out, out, out, jax.ShapeDtypeStruct((bsz, RW_HEADS, RW_HEAD, RW_HEAD), F32)),
        grid=(bsz // nblk, t // chunk),
        in_specs=[pl.BlockSpec((nblk, chunk, RW_COLS), lambda i, j: (i, j, 0)),
                  pl.BlockSpec((nblk, 1, RW_COLS), lambda i, j: (i, 0, 0)), st,
                  full((1, RW_COLS)), full((1, RW_WIDTH)), full((LORA, RW_WIDTH)),
                  full((1, RW_WIDTH)), full((LORA, RW_WIDTH)), full((1, RW_WIDTH)),
                  full((1, RW_WIDTH)), full((1, RW_WIDTH)), full((RW_WIDTH, RW_WIDTH))],
        out_specs=(seq, seq, seq, st),
        scratch_shapes=[pltpu.VMEM((nblk, N_PAIRS, PAIR, PAIR), F32),
                        pltpu.VMEM((nblk, 1, RW_COLS), F32)],
        compiler_params=_cparams(("arbitrary", "arbitrary")),
        name="rwkv_mix",
    )(p3, prev, s0, row(mu), row(w0), w2.astype(BF16), row(a0), a2.astype(BF16), row(k_k), row(k_a),
      row(r_k), e)


def _da_prep_body(p_ref, qg_ref, kg_ref, e_ref, q_ref, k_ref, *kv_refs, tm):
    q = p_ref[:, 0:512]
    k = p_ref[:, 512:1024]
    qm = _head_sum(q * q, e_ref) * (1.0 / DA_QK)
    km = _head_sum(k * k, e_ref) * (1.0 / DA_QK)
    qn = q * lax.rsqrt(qm + NORM_EPS) * qg_ref[...]
    kn = k * lax.rsqrt(km + NORM_EPS) * kg_ref[...]
    q_ref[...] = (qn * (DA_QK ** -0.5)).astype(BF16)
    k_ref[...] = kn
    if kv_refs:
        kb_ref, vt_ref = kv_refs
        kb_ref[...] = kn.astype(BF16)
        for j in range(tm // SEQ_TILE):
            vt_ref[j] = p_ref[j * SEQ_TILE:(j + 1) * SEQ_TILE, 1024:1536].T.astype(BF16)


def _da_prep(pda, qg, kg, e, tm, with_kv):
    n = pda.shape[0]
    tile8 = lambda g: jnp.tile(g, DA_WIDTH // DA_QK).reshape(1, DA_WIDTH)
    rows = pl.BlockSpec((tm, DA_WIDTH), lambda i: (i, 0))
    bf = jax.ShapeDtypeStruct((n, DA_WIDTH), BF16)
    out_shape = (bf, jax.ShapeDtypeStruct((n, DA_WIDTH), F32))
    out_specs = (rows, rows)
    if with_kv:
        out_shape += (bf, jax.ShapeDtypeStruct((n // SEQ_TILE, DA_WIDTH, SEQ_TILE), BF16))
        out_specs += (rows, pl.BlockSpec((tm // SEQ_TILE, DA_WIDTH, SEQ_TILE), lambda i: (i, 0, 0)))
    return pl.pallas_call(
        functools.partial(_da_prep_body, tm=tm),
        out_shape=out_shape,
        grid=(n // tm,),
        in_specs=[pl.BlockSpec((tm, DA_COLS), lambda i: (i, 0)),
                  pl.BlockSpec((1, DA_WIDTH), lambda i: (0, 0)),
                  pl.BlockSpec((1, DA_WIDTH), lambda i: (0, 0)),
                  pl.BlockSpec((RW_WIDTH, RW_WIDTH), lambda i: (0, 0))],
        out_specs=out_specs,
        compiler_params=_cparams(("arbitrary",)),
        name="da_prep",
    )(pda, tile8(qg), tile8(kg), e)


def _bucket_starts():
    max_exact = N_BUCKETS // 2
    n = np.arange(1, MAX_DISTANCE + 1)
    val = (np.log(n.astype(np.float32) / np.float32(max_exact))
           / np.float32(math.log(MAX_DISTANCE / max_exact)) * np.float32(N_BUCKETS - max_exact))
    inner = (n > max_exact) & (n < MAX_DISTANCE)
    assert np.all(np.abs(val[inner] - np.round(val[inner])) > 1e-3)
    bucket = np.where(n < max_exact, n, np.minimum(max_exact + val.astype(np.int32), N_BUCKETS - 1))
    return [int(n[np.argmax(bucket >= b)]) for b in range(1, N_BUCKETS)]


def _bias_tiles(n, rel_ref):
    accs = [jnp.full(n.shape, rel_ref[0, h], F32) for h in range(DA_HEADS)]
    for b, start in enumerate(_bucket_starts(), start=1):
        in_or_past = n >= start
        accs = [jnp.where(in_or_past, rel_ref[b, h], acc) for h, acc in enumerate(accs)]
    return [jnp.where(n >= 0, acc, NEG_INF) for acc in accs]


def _lambda(lam_ref, layer):
    lq1 = lam_ref[0:1, :]
    lk1 = lam_ref[1:2, :]
    lq2 = lam_ref[2:3, :]
    lk2 = lam_ref[3:4, :]
    li = 0.8 - 0.6 * math.exp(-0.3 * layer)
    lam = (jnp.exp(jnp.sum(lq1 * lk1, axis=1, keepdims=True))
           - jnp.exp(jnp.sum(lq2 * lk2, axis=1, keepdims=True)) + li)
    return lam, li


def _softmax_update(carries, q_sts, kbs, vts, biases):
    n = len(q_sts)
    score = lambda h: (lax.dot_general(kbs[h], q_sts[h], _QK_DIMS, preferred_element_type=F32)
                       + biases[h])
    out = []
    s_next = score(0)
    for h in range(n):
        s = s_next
        if h + 1 < n:
            s_next = score(h + 1)
        m, l, acc = carries[h]
        m_new = jnp.maximum(m, jnp.max(s, axis=0, keepdims=True))
        alpha = jnp.exp(m - m_new)
        p = jnp.exp(s - m_new)
        l = alpha * l + jnp.sum(p, axis=0, keepdims=True)
        acc = alpha * acc + jnp.dot(vts[h], p.astype(BF16), preferred_element_type=F32)
        out.append((m_new, l, acc))
    return tuple(out)


def _head_out(o_st, z, lam, li, sg_ref, tq):
    o = o_st[:tq] - lam * o_st[tq:]
    ms = jnp.mean(o * o, axis=1, keepdims=True)
    return o * lax.rsqrt(ms + NORM_EPS) * sg_ref[...] * (1.0 - li) * (z * jax.nn.sigmoid(z))


def _split_maps(qh):
    lane = lax.broadcasted_iota(jnp.int32, qh.shape, 1)
    zero = jnp.zeros_like(qh)
    return jnp.where(lane < DA_QK, qh, zero), jnp.where(lane >= DA_QK, qh, zero)


_QK_DIMS = (((1,), (1,)), ((), ()))


def _attn_prompt_body(q_ref, k_ref, vt_ref, z_ref, rel_ref, lam_ref, sg_ref, o_ref, bias_ref,
                      *, tq, layer):
    qi = pl.program_id(1)

    @pl.when((pl.program_id(0) == 0) & (qi == 0))
    def _():
        d = (lax.broadcasted_iota(jnp.int32, (tq, tq), 1)
             - lax.broadcasted_iota(jnp.int32, (tq, tq), 0))
        for blk in range(2):
            tiles = _bias_tiles(d + blk * tq, rel_ref)
            for h in range(DA_HEADS):
                bias_ref[blk * DA_HEADS + h] = tiles[h]

    lam, li = _lambda(lam_ref, layer)
    n_far = jnp.maximum(qi - 1, 0)
    heads = range(DA_HEADS)
    sls = [slice(h * DA_V, (h + 1) * DA_V) for h in heads]
    q_sts = [jnp.concatenate(_split_maps(q_ref[:, sl]), axis=0) for sl in sls]
    far_bias = [rel_ref[N_BUCKETS - 1, h] for h in heads]

    def attend(carries, kblk, nblk, biases):
        off = pl.multiple_of(kblk * tq, tq)
        vts = [jnp.concatenate([vt_ref[kblk + i, sl, :] for i in range(nblk)], axis=1)
               for sl in sls]
        return _softmax_update(carries, q_sts, [k_ref[pl.ds(off, nblk * tq), sl] for sl in sls],
                               vts, biases)

    init = (jnp.full((1, 2 * tq), NEG_INF, F32), jnp.zeros((1, 2 * tq), F32),
            jnp.zeros((DA_V, 2 * tq), F32))
    carries = lax.fori_loop(0, n_far >> 1, lambda j, c: attend(c, 2 * j, 2, far_bias),
                            (init,) * DA_HEADS)
    carries = lax.cond((n_far & 1) == 1, lambda c: attend(c, n_far - 1, 1, far_bias),
                       lambda c: c, carries)
    first = qi == 0
    near = []
    for h in heads:
        top = jnp.where(first, bias_ref[h], bias_ref[DA_HEADS + h])
        bottom = bias_ref[h] + jnp.where(first, NEG_INF, 0.0)
        tile = jnp.concatenate([top, bottom], axis=0)
        near.append(jnp.concatenate([tile, tile], axis=1))
    carries = attend(carries, n_far, 2, near)
    for h in heads:
        _, l, acc = carries[h]
        o_t = acc / l
        o_st = jnp.concatenate([o_t[:, :tq].T, o_t[:, tq:].T], axis=0)
        o_ref[:, sls[h]] = _head_out(o_st, z_ref[:, sls[h]], lam, li, sg_ref, tq)


def _attn_prompt(qn, kb, vt, pda, rel_bias, lam, sg, layer, tq):
    b, t, _ = qn.shape
    nblk = t // tq
    return pl.pallas_call(
        functools.partial(_attn_prompt_body, tq=tq, layer=layer),
        out_shape=jax.ShapeDtypeStruct((b, t, DA_WIDTH), F32),
        grid=(b, nblk),
        in_specs=[pl.BlockSpec((None, tq, DA_WIDTH), lambda i, q: (i, q, 0)),
                  pl.BlockSpec((None, t, DA_WIDTH), lambda i, q: (i, 0, 0)),
                  pl.BlockSpec((nblk, DA_WIDTH, tq), lambda i, q: (i, 0, 0)),
                  pl.BlockSpec((None, tq, DA_WIDTH), lambda i, q: (i, q, 3)),
                  pl.BlockSpec(memory_space=pltpu.SMEM),
                  pl.BlockSpec((4, DA_QK), lambda i, q: (0, 0)),
                  pl.BlockSpec((1, DA_V), lambda i, q: (0, 0))],
        out_specs=pl.BlockSpec((None, tq, DA_WIDTH), lambda i, q: (i, q, 0)),
        scratch_shapes=[pltpu.VMEM((2 * DA_HEADS, tq, tq), F32)],
        compiler_params=_cparams(("arbitrary", "arbitrary")),
        name="attn_prompt",
    )(qn, kb, vt, pda, rel_bias, lam, sg.reshape(1, DA_V))


def _attn_sample_body(pt_ref, q_ref, *refs, tq, npg, layer):
    kc = refs[:npg]
    vc = refs[npg:2 * npg]
    kn_ref, vn_ref, z_ref, rel_ref, lam_ref, sg_ref, o_ref, bias_ref = refs[2 * npg:]
    past = npg * PAGE_SIZE
    ncol = past + PAGE_SIZE

    @pl.when(pl.program_id(0) == 0)
    def _():
        t_idx = lax.rem(lax.broadcasted_iota(jnp.int32, (2 * tq, ncol), 0), tq)
        col = lax.broadcasted_iota(jnp.int32, (2 * tq, ncol), 1)
        tiles = _bias_tiles(past + t_idx - col, rel_ref)
        for h in range(DA_HEADS):
            bias_ref[h] = tiles[h]

    lam, li = _lambda(lam_ref, layer)
    pad = jnp.zeros((PAGE_SIZE - tq, DA_V), F32)
    outs = []
    for h in range(DA_HEADS):
        sl = slice(h * DA_V, (h + 1) * DA_V)
        q_st = jnp.concatenate(_split_maps(q_ref[:, sl]), axis=0)
        rows_h = pl.ds(h, PAGE_SIZE, stride=DA_HEADS)
        page_h = lambda ref: ref.reshape(PAGE_SIZE * DA_HEADS, DA_V)[rows_h, :]
        kh = jnp.concatenate([page_h(kc[j]) for j in range(npg)] + [kn_ref[:, sl], pad], axis=0)
        vh = jnp.concatenate([page_h(vc[j]) for j in range(npg)] + [vn_ref[:, sl], pad], axis=0)
        s = lax.dot_general(q_st, kh.astype(BF16), _QK_DIMS, preferred_element_type=F32)
        s = s + bias_ref[h]
        m = jnp.max(s, axis=1, keepdims=True)
        p = jnp.exp(s - m)
        l = jnp.sum(p, axis=1, keepdims=True)
        o = jnp.dot(p.astype(BF16), vh.astype(BF16), preferred_element_type=F32) / l
        outs.append(_head_out(o, z_ref[:, sl], lam, li, sg_ref, tq))
    o_ref[...] = jnp.concatenate(outs, axis=1)


def _attn_sample(qn, kn, pda, cache_k, cache_v, page_table, rel_bias, lam, sg, layer):
    b, tq, _ = qn.shape
    npg = page_table.shape[1]
    page_specs = [pl.BlockSpec((None, None, PAGE_SIZE, DA_HEADS, DA_V),
                               lambda i, pt, j=j: (layer, pt[i, j], 0, 0, 0)) for j in range(npg)]
    seq_spec = lambda col: pl.BlockSpec((None, tq, DA_WIDTH), lambda i, pt: (i, 0, col))
    grid_spec = pltpu.PrefetchScalarGridSpec(
        num_scalar_prefetch=1,
        grid=(b,),
        in_specs=[seq_spec(0)] + page_specs + page_specs
                 + [seq_spec(0), seq_spec(2), seq_spec(3),
                    pl.BlockSpec(memory_space=pltpu.SMEM),
                    pl.BlockSpec((4, DA_QK), lambda i, pt: (0, 0)),
                    pl.BlockSpec((1, DA_V), lambda i, pt: (0, 0))],
        out_specs=pl.BlockSpec((None, tq, DA_WIDTH), lambda i, pt: (i, 0, 0)),
        scratch_shapes=[pltpu.VMEM((DA_HEADS, 2 * tq, (npg + 1) * PAGE_SIZE), F32)],
    )
    return pl.pallas_call(
        functools.partial(_attn_sample_body, tq=tq, npg=npg, layer=layer),
        out_shape=jax.ShapeDtypeStruct((b, tq, DA_WIDTH), F32),
        grid_spec=grid_spec,
        compiler_params=_cparams(("arbitrary",)),
        name="attn_sample",
    )(page_table, qn, *([cache_k] * npg), *([cache_v] * npg), kn, pda, pda, rel_bias, lam,
      sg.reshape(1, DA_V))


def _out_proj_body(x_ref, y_ref, sz_ref, bz_ref, yb_ref, g_ref, lg_ref, lb_ref, e_ref,
                   wa_ref, wb_ref, wo_ref, o_ref):
    y = y_ref[...]
    mean = _head_sum(y, e_ref) * (1.0 / RW_HEAD)
    yc = y - mean
    var = _head_sum(yc * yc, e_ref) * (1.0 / RW_HEAD)
    yn = yc * lax.rsqrt(var + GN_EPS) * lg_ref[...] + lb_ref[...]
    ya = yn * sz_ref[...] + bz_ref[...]
    ga = jax.nn.sigmoid(g_ref[:, 0:D_MODEL])
    gb = jax.nn.sigmoid(g_ref[:, D_MODEL:2 * D_MODEL])
    m = (ga * jnp.dot(ya.astype(BF16), wa_ref[...], preferred_element_type=F32)
         + gb * jnp.dot(yb_ref[...].astype(BF16), wb_ref[...], preferred_element_type=F32))
    o_ref[...] = x_ref[...] + jnp.dot(m.astype(BF16), wo_ref[...], preferred_element_type=F32)


def _out_proj(x2d, y, sz, bz, yb, pg, lnx_g, lnx_b, e, wa, wb, wo, tm):
    n = x2d.shape[0]
    rows = lambda c: pl.BlockSpec((tm, c), lambda i: (i, 0))
    full = lambda r, c: pl.BlockSpec((r, c), lambda i: (0, 0))
    return pl.pallas_call(
        _out_proj_body,
        out_shape=jax.ShapeDtypeStruct((n, D_MODEL), F32),
        grid=(n // tm,),
        in_specs=[rows(D_MODEL), rows(RW_WIDTH), rows(RW_WIDTH), rows(RW_WIDTH), rows(DA_WIDTH),
                  rows(G_COLS), full(1, RW_WIDTH), full(1, RW_WIDTH), full(RW_WIDTH, RW_WIDTH),
                  full(RW_WIDTH, D_MODEL), full(DA_WIDTH, D_MODEL), full(D_MODEL, D_MODEL)],
        out_specs=rows(D_MODEL),
        compiler_params=_cparams(("arbitrary",)),
        name="out_proj",
    )(x2d, y, sz, bz, yb, pg, lnx_g.reshape(1, -1), lnx_b.reshape(1, -1), e, wa, wb, wo)


def _trunk(x, t_valid, tm, tt, shift0, state0, cache_k, cache_v, page_table, prm):
    b, t, _ = x.shape
    n = b * t
    e = (jnp.arange(RW_WIDTH)[:, None] // RW_HEAD == jnp.arange(RW_WIDTH)[None, :] // RW_HEAD)
    e = e.astype(BF16)
    new_k, new_v, new_s, new_shift = [], [], [], []
    for l in range(2):
        w_in = prm["w_in"][l]
        w_rw = w_in[:, :RW_COLS].astype(BF16)
        x2d = x.reshape(n, D_MODEL)
        tp = 2 * tm if n % (2 * tm) == 0 else tm
        p_rw = _in_proj(x2d, prm["norm_g"][l], w_rw, tp)
        pda = _in_proj(x2d, prm["norm_g"][l], w_in[:, RW_COLS:RW_COLS + DA_COLS].astype(BF16), tp)
        pg = _in_proj(x2d, prm["norm_g"][l], w_in[:, RW_COLS + DA_COLS:].astype(BF16), tp)
        new_shift.append(_norm_rows(x[:, t_valid - 1], prm["norm_g"][l]))

        if shift0 is None:
            prev = jnp.zeros((b, 1, RW_COLS), F32)
            s_init = jnp.zeros((b, RW_HEADS, RW_HEAD, RW_HEAD), F32)
        else:
            prev = _in_proj(shift0[l], prm["norm_g"][l], w_rw, shift0[l].shape[0],
                            normalize=False).reshape(b, 1, RW_COLS)
            s_init = state0[l]
        y, sz, bz, s_fin = _rwkv_mix(
            p_rw.reshape(b, t, RW_COLS), prev, s_init, prm["mu_shift"][l], prm["w0"][l],
            prm["w2"][l], prm["a0"][l], prm["a2"][l], prm["k_k"][l], prm["k_a"][l], prm["r_k"][l],
            e, t_valid)

        qn, kn, *kv = _da_prep(pda, prm["q_norm_g"][l], prm["k_norm_g"][l], e, tm, cache_k is None)
        lam = jnp.stack([prm["lam_q1"][l], prm["lam_k1"][l], prm["lam_q2"][l], prm["lam_k2"][l]])
        pda3 = pda.reshape(b, t, DA_COLS)
        qn3 = qn.reshape(b, t, DA_WIDTH)
        kn3 = kn.reshape(b, t, DA_WIDTH)
        if cache_k is None:
            yb = _attn_prompt(qn3, kv[0].reshape(b, t, DA_WIDTH), kv[1], pda3, prm["rel_bias"], lam,
                              prm["subln_g"][l], l, tt)
        else:
            yb = _attn_sample(qn3, kn3, pda3, cache_k, cache_v, page_table, prm["rel_bias"], lam,
                              prm["subln_g"][l], l)

        x = _out_proj(x2d, y.reshape(n, RW_WIDTH), sz.reshape(n, RW_WIDTH),
                      bz.reshape(n, RW_WIDTH), yb.reshape(n, DA_WIDTH), pg, prm["lnx_g"][l],
                      prm["lnx_b"][l], e, prm["w_a_out"][l].astype(BF16),
                      prm["w_b_out"][l].astype(BF16), prm["w_o"][l].astype(BF16),
                      tm).reshape(b, t, D_MODEL)
        new_k.append(kn3[:, :t_valid].reshape(b, t_valid, DA_HEADS, DA_V))
        new_v.append(pda3[:, :t_valid, 2 * DA_WIDTH:3 * DA_WIDTH].reshape(b, t_valid, DA_HEADS, DA_V))
        new_s.append(s_fin)
    return x, jnp.stack(new_k), jnp.stack(new_v), jnp.stack(new_s), jnp.stack(new_shift)


def kernel(x_prompt, x_sample, cache_k, cache_v, page_table, state_rwkv, state_shift, meta_tokens,
           rel_bias, norm_g, w_in, mu_shift, w0, w2, a0, a2, k_k, k_a, r_k, lnx_g, lnx_b,
           q_norm_g, k_norm_g, lam_q1, lam_k1, lam_q2, lam_k2, subln_g, w_a_out, w_b_out, w_o):
    prm = dict(rel_bias=rel_bias, norm_g=norm_g, w_in=w_in, mu_shift=mu_shift, w0=w0, w2=w2,
               a0=a0, a2=a2, k_k=k_k, k_a=k_a, r_k=r_k.reshape(2, RW_WIDTH), lnx_g=lnx_g,
               lnx_b=lnx_b, q_norm_g=q_norm_g, k_norm_g=k_norm_g, lam_q1=lam_q1, lam_k1=lam_k1,
               lam_q2=lam_q2, lam_k2=lam_k2, subln_g=subln_g, w_a_out=w_a_out, w_b_out=w_b_out,
               w_o=w_o)
    bp, seq, _ = x_prompt.shape
    t_valid = seq + N_META
    t_pad = -(-t_valid // SEQ_TILE) * SEQ_TILE
    meta = jnp.broadcast_to(meta_tokens[None], (bp, N_META, D_MODEL))
    xp = jnp.concatenate([meta, x_prompt, jnp.zeros((bp, t_pad - t_valid, D_MODEL), F32)], axis=1)
    yp, kp, vp, sp, hp = _trunk(xp, t_valid, 512, SEQ_TILE, None, None, None, None, None, prm)

    ts = x_sample.shape[1]
    ys, k_s, v_s, s_s, h_s = _trunk(x_sample, ts, 512, ts, state_shift, state_rwkv, cache_k,
                                    cache_v, page_table, prm)
    return (yp[:, N_META:t_valid], ys, kp, vp, sp, hp, k_s, v_s, s_s, h_s)
```

```python
import functools
import math

import jax
import jax.numpy as jnp
import numpy as np
from jax import lax
from jax.experimental import pallas as pl
from jax.experimental.pallas import tpu as pltpu

F32 = jnp.float32
BF16 = jnp.bfloat16

D_MODEL = 1024
N_META = 16
PAGE_SIZE = 128
RW_WIDTH = 512
RW_HEAD = 64
RW_HEADS = 8
LORA = 64
GN_EPS = 64e-5
DA_WIDTH = 512
DA_QK = 64
DA_HEADS = 4
DA_V = 128
N_BUCKETS = 32
MAX_DISTANCE = 128
NORM_EPS = 1e-6
NEG_INF = -1e30

RW_COLS = 3 * RW_WIDTH + 2 * LORA + RW_WIDTH
DA_COLS = 4 * DA_WIDTH
G_COLS = 2 * D_MODEL

SEQ_TILE = 128
VMEM_LIMIT = 48 * 1024 * 1024


def _cparams(sem):
    return pltpu.CompilerParams(dimension_semantics=sem, vmem_limit_bytes=VMEM_LIMIT)


def _head_sum(x, e_ref):
    hi = x.astype(BF16)
    lo = (x - hi.astype(F32)).astype(BF16)
    e = e_ref[...]
    return (jnp.dot(hi, e, preferred_element_type=F32)
            + jnp.dot(lo, e, preferred_element_type=F32))


def _in_proj_body(x_ref, g_ref, w_ref, o_ref, *, normalize):
    x = x_ref[...]
    if normalize:
        ms = jnp.mean(x * x, axis=-1, keepdims=True)
        x = x * lax.rsqrt(ms + NORM_EPS) * g_ref[...]
    o_ref[...] = jnp.dot(x.astype(BF16), w_ref[...], preferred_element_type=F32)


def _in_proj(x2d, g, w, tm, normalize=True):
    n, d = x2d.shape
    c = w.shape[1]
    return pl.pallas_call(
        functools.partial(_in_proj_body, normalize=normalize),
        out_shape=jax.ShapeDtypeStruct((n, c), F32),
        grid=(n // tm,),
        in_specs=[pl.BlockSpec((tm, d), lambda i: (i, 0)),
                  pl.BlockSpec((1, d), lambda i: (0, 0)),
                  pl.BlockSpec((d, c), lambda i: (0, 0))],
        out_specs=pl.BlockSpec((tm, c), lambda i: (i, 0)),
        compiler_params=_cparams(("arbitrary",)),
        name="in_proj",
    )(x2d, g.reshape(1, d), w)


def _norm_rows_body(x_ref, g_ref, o_ref):
    x = x_ref[...]
    ms = jnp.mean(x * x, axis=-1, keepdims=True)
    o_ref[...] = x * lax.rsqrt(ms + NORM_EPS) * g_ref[...]


def _norm_rows(x2d, g):
    n, d = x2d.shape
    return pl.pallas_call(
        _norm_rows_body,
        out_shape=jax.ShapeDtypeStruct((n, d), F32),
        name="norm_rows",
    )(x2d, g.reshape(1, d))


def _softplus(x):
    return jnp.maximum(x, 0.0) + jnp.log1p(jnp.exp(-jnp.abs(x)))


def _rwkv_prep(p, shifted, mu_ref, w0_ref, w2_ref, a0_ref, a2_ref, kk_ref, ka_ref, rk_ref, e_ref):
    h = p + (shifted - p) * mu_ref[...]
    r = h[:, 0:512]
    k = h[:, 512:1024]
    v = h[:, 1024:1536]
    wd = h[:, 1536:1600]
    ad = h[:, 1600:1664]
    z = h[:, 1664:2176]

    lw = w0_ref[...] + jnp.dot(jnp.tanh(wd).astype(BF16), w2_ref[...], preferred_element_type=F32)
    log_decay = -jnp.exp(-_softplus(-lw) - 0.5)
    rate = jax.nn.sigmoid(a0_ref[...] + jnp.dot(ad.astype(BF16), a2_ref[...],
                                                 preferred_element_type=F32))
    kk = k * kk_ref[...]
    ss = _head_sum(kk * kk, e_ref)
    kk = kk / jnp.maximum(jnp.sqrt(ss), 1e-12)
    kmod = k * (1.0 + (rate - 1.0) * ka_ref[...])
    bonus = _head_sum(r * kmod * rk_ref[...], e_ref) * v
    sz = z * jax.nn.sigmoid(z)
    return r, log_decay, kmod, v, -kk, kk * rate, sz, bonus * sz


PAIR = 2 * RW_HEAD
N_PAIRS = RW_HEADS // 2
SCAN_CHUNK = 64
SCAN_GROUPS = 4
INV_BASE = 8


def _mm(a, b):
    return jnp.dot(a.astype(BF16), b.astype(BF16), preferred_element_type=F32)


def _seg_cumsum(x, seg, reverse=False):
    n = x.shape[0]
    t = lax.rem(lax.broadcasted_iota(jnp.int32, x.shape, 0), seg)
    y = x
    s = 1
    while s < seg:
        if reverse:
            y = y + jnp.where(t < seg - s, pltpu.roll(y, n - s, axis=0), 0.0)
        else:
            y = y + jnp.where(t >= s, pltpu.roll(y, s, axis=0), 0.0)
        s *= 2
    return y - x if reverse else y


def _rwkv_mix_body(p_ref, prev_ref, s0_ref, mu_ref, w0_ref, w2_ref, a0_ref, a2_ref, kk_ref, ka_ref,
                   rk_ref, e_ref, y_ref, sz_ref, bz_ref, st_ref, h_ref, carry_ref,
                   *, ngrp, nseq, chunk, t_valid):
    c = pl.program_id(1)
    nblk = ngrp * nseq
    rows = nseq * chunk
    n2 = 2 * rows
    lane = lax.broadcasted_iota(jnp.int32, (rows, PAIR), 1)
    lo_half = lane < RW_HEAD

    @pl.when(c == 0)
    def _():
        carry_ref[...] = prev_ref[...]
        z = jnp.zeros((RW_HEAD, RW_HEAD), F32)
        for s in range(nblk):
            for p in range(N_PAIRS):
                blk = jnp.concatenate([jnp.concatenate([s0_ref[s, 2 * p], z], axis=1),
                                       jnp.concatenate([z, s0_ref[s, 2 * p + 1]], axis=1)], axis=0)
                h_ref[s, p] = blk.T

    p3 = p_ref[...]
    p_all = p3.reshape(nblk * chunk, RW_COLS)
    t_all = lax.rem(lax.broadcasted_iota(jnp.int32, p_all.shape, 0), chunk)
    carried = jnp.broadcast_to(carry_ref[...], p3.shape).reshape(p_all.shape)
    shifted = jnp.where(t_all == 0, carried, pltpu.roll(p_all, 1, axis=0))
    carry_ref[...] = p3[:, chunk - 1:chunk, :]
    r_all, lw_all, k_all, v_all, a_all, b_all, sz, bz = _rwkv_prep(
        p_all, shifted, mu_ref, w0_ref, w2_ref, a0_ref, a2_ref, kk_ref, ka_ref, rk_ref, e_ref)
    sz_ref[...] = sz.reshape(sz_ref.shape)
    bz_ref[...] = bz.reshape(bz_ref.shape)

    tpos = c * chunk + lax.rem(lax.broadcasted_iota(jnp.int32, (rows, RW_WIDTH), 0), chunk)
    valid = tpos < t_valid
    sls = [slice(p * PAIR, (p + 1) * PAIR) for p in range(N_PAIRS)]
    stack = lambda x: [jnp.concatenate([jnp.where(lo_half, x[:, sl], 0.0),
                                        jnp.where(lo_half, 0.0, x[:, sl])], axis=0) for sl in sls]
    a_st, r_st, b_st, k_st, bh_st, kh_st, v_st, e_pos = [], [], [], [], [], [], [], []
    for grp in range(ngrp):
        msk = lambda x: jnp.where(valid, x[grp * rows:(grp + 1) * rows], 0.0)
        r, lw, k, v, a, b = (msk(x) for x in (r_all, lw_all, k_all, v_all, a_all, b_all))
        g = _seg_cumsum(lw, chunk)
        e_neg = jnp.exp(-g)
        e_end = jnp.exp(_seg_cumsum(lw, chunk, reverse=True))
        e_pos.append(jnp.exp(g))
        a_st += stack(a * jnp.exp(g - lw))
        r_st += stack(r * e_pos[grp])
        b_st += stack(b * e_neg)
        k_st += stack(k * e_neg)
        bh_st += stack(b * e_end)
        kh_st += stack(k * e_end)
        v_st += stack(v)

    row2 = lax.broadcasted_iota(jnp.int32, (n2, n2), 0)
    col2 = lax.broadcasted_iota(jnp.int32, (n2, n2), 1)
    same_seq = (row2 // chunk) == (col2 // chunk)
    strict = same_seq & (row2 > col2)
    incl = same_seq & (row2 >= col2)
    zeros = jnp.zeros((n2, PAIR), F32)
    seq_of_row = lax.rem(lax.broadcasted_iota(jnp.int32, (n2, PAIR), 0), rows) // chunk
    seq_of_col = lax.rem(lax.broadcasted_iota(jnp.int32, (PAIR, 2 * n2), 1), rows) // chunk

    probs = range(ngrp * N_PAIRS)
    each = lambda f, *xs: [f(*(x[i] for x in xs)) for i in probs]
    gram = each(lambda ar, bk: lax.dot_general(ar.astype(BF16), bk.astype(BF16), _QK_DIMS,
                                               preferred_element_type=F32),
                each(lambda x, y: jnp.concatenate([x, y], axis=0), a_st, r_st),
                each(lambda x, y: jnp.concatenate([x, y], axis=0), b_st, k_st))
    mab = each(lambda g_: jnp.where(strict, g_[:n2, :n2], 0.0), gram)
    mak = each(lambda g_: jnp.where(strict, g_[:n2, n2:], 0.0), gram)
    mrbk = each(lambda g_: jnp.concatenate([jnp.where(incl, g_[n2:, :n2], 0.0),
                                            jnp.where(incl, g_[n2:, n2:], 0.0)], axis=1), gram)
    x1 = each(_mm, mak, v_st)

    same = lambda w: (row2 // w) == (col2 // w)
    md = each(lambda m: jnp.where(same(INV_BASE), m, 0.0), mab)
    m2 = each(_mm, md, md)
    m4 = each(_mm, m2, m2)
    t_inv = each(lambda m: (row2 == col2).astype(F32) + m, md)
    t_inv = each(lambda t, m: t + _mm(t, m), t_inv, m2)
    t_inv = each(lambda t, m: t + _mm(t, m), t_inv, m4)
    width = INV_BASE
    while width < chunk:
        off = each(lambda m: jnp.where(same(2 * width) & jnp.logical_not(same(width)), m, 0.0), mab)
        t_off = each(_mm, t_inv, off)
        t_inv = each(lambda t, to: t + _mm(to, t), t_inv, t_off)
        width *= 2

    tx = each(lambda t, x, a_: _mm(t, jnp.concatenate([x, a_], axis=1)), t_inv, x1, a_st)
    rh = each(lambda t, v_: jnp.concatenate(
        [jnp.concatenate([t[:, PAIR:], t[:, :PAIR]], axis=1),
         jnp.concatenate([zeros, v_], axis=1)], axis=0), tx, v_st)
    qy = each(_mm, mrbk, rh)
    q = each(lambda r_, qy_: r_ + qy_[:, :PAIR], r_st, qy)
    yst = each(lambda qy_: qy_[:, PAIR:], qy)
    bk_t = each(lambda x, y: jnp.concatenate([x, y], axis=0).T, bh_st, kh_st)
    only = nseq > 1
    for s in range(nseq):
        slot = lambda i: (i // N_PAIRS * nseq + s, i % N_PAIRS)
        h = [h_ref[slot(i)] for i in probs]
        yst = each(lambda y_, q_, h_: y_ + _mm(jnp.where(seq_of_row == s, q_, 0.0) if only else q_,
                                               h_), yst, q, h)
        gc = each(lambda b_, r_: _mm(jnp.where(seq_of_col == s, b_, 0.0) if only else b_, r_),
                  bk_t, rh)
        gh = each(lambda g_, h_: _mm(g_[:, :PAIR], h_), gc, h)
        last = s * chunk + chunk - 1
        for i in probs:
            g_end = e_pos[i // N_PAIRS][last:last + 1, sls[i % N_PAIRS]]
            g_col = jnp.broadcast_to(g_end, (PAIR, PAIR)).T
            h_ref[slot(i)] = g_col * h[i] + gh[i] + gc[i][:, PAIR:]
    for grp in range(ngrp):
        y_grp = [y_[:rows] + y_[rows:] for y_ in yst[grp * N_PAIRS:(grp + 1) * N_PAIRS]]
        y_ref[grp * nseq:(grp + 1) * nseq] = jnp.concatenate(y_grp, axis=1).reshape(
            nseq, chunk, RW_WIDTH)

    @pl.when(c == pl.num_programs(1) - 1)
    def _():
        for s in range(ngrp * nseq):
            for p in range(N_PAIRS):
                ht = h_ref[s, p].T
                st_ref[s, 2 * p] = ht[:RW_HEAD, :RW_HEAD]
                st_ref[s, 2 * p + 1] = ht[RW_HEAD:, RW_HEAD:]


def _rwkv_mix(p3, prev, s0, mu, w0, w2, a0, a2, k_k, k_a, r_k, e, t_valid):
    bsz, t, _ = p3.shape
    chunk = min(t, SCAN_CHUNK)
    nseq = SCAN_CHUNK // chunk
    ngrp = SCAN_GROUPS if nseq == 1 else 2
    nblk = ngrp * nseq
    row = lambda x: x.reshape(1, -1)
    seq = pl.BlockSpec((nblk, chunk, RW_WIDTH), lambda i, j: (i, j, 0))
    st = pl.BlockSpec((nblk, RW_HEADS, RW_HEAD, RW_HEAD), lambda i, j: (i, 0, 0, 0))
    full = lambda shape: pl.BlockSpec(shape, lambda i, j: (0,) * len(shape))
    out = jax.ShapeDtypeStruct((bsz, t, RW_WIDTH), F32)
    return pl.pallas_call(
        functools.partial(_rwkv_mix_body, ngrp=ngrp, nseq=nseq, chunk=chunk, t_valid=t_valid),
        out_shape=(out, out, out, jax.ShapeDtypeStruct((bsz, RW_HEADS, RW_HEAD, RW_HEAD), F32)),
        grid=(bsz // nblk, t // chunk),
        in_specs=[pl.BlockSpec((nblk, chunk, RW_COLS), lambda i, j: (i, j, 0)),
                  pl.BlockSpec((nblk, 1, RW_COLS), lambda i, j: (i, 0, 0)), st,
                  full((1, RW_COLS)), full((1, RW_WIDTH)), full((LORA, RW_WIDTH)),
                  full((1, RW_WIDTH)), full((LORA, RW_WIDTH)), full((1, RW_WIDTH)),
                  full((1, RW_WIDTH)), full((1, RW_WIDTH)), full((RW_WIDTH, RW_WIDTH))],
        out_specs=(seq, seq, seq, st),
        scratch_shapes=[pltpu.VMEM((nblk, N_PAIRS, PAIR, PAIR), F32),
                        pltpu.VMEM((nblk, 1, RW_COLS), F32)],
        compiler_params=_cparams(("arbitrary", "arbitrary")),
        name="rwkv_mix",
    )(p3, prev, s0, row(mu), row(w0), w2.astype(BF16), row(a0), a2.astype(BF16), row(k_k), row(k_a),
      row(r_k), e)


def _da_prep_body(p_ref, qg_ref, kg_ref, e_ref, q_ref, k_ref, *kv_refs, tm):
    q = p_ref[:, 0:512]
    k = p_ref[:, 512:1024]
    qm = _head_sum(q * q, e_ref) * (1.0 / DA_QK)
    km = _head_sum(k * k, e_ref) * (1.0 / DA_QK)
    qn = q * lax.rsqrt(qm + NORM_EPS) * qg_ref[...]
    kn = k * lax.rsqrt(km + NORM_EPS) * kg_ref[...]
    q_ref[...] = (qn * (DA_QK ** -0.5)).astype(BF16)
    k_ref[...] = kn
    if kv_refs:
        kb_ref, vt_ref = kv_refs
        kb_ref[...] = kn.astype(BF16)
        for j in range(tm // SEQ_TILE):
            vt_ref[j] = p_ref[j * SEQ_TILE:(j + 1) * SEQ_TILE, 1024:1536].T.astype(BF16)


def _da_prep(pda, qg, kg, e, tm, with_kv):
    n = pda.shape[0]
    tile8 = lambda g: jnp.tile(g, DA_WIDTH // DA_QK).reshape(1, DA_WIDTH)
    rows = pl.BlockSpec((tm, DA_WIDTH), lambda i: (i, 0))
    bf = jax.ShapeDtypeStruct((n, DA_WIDTH), BF16)
    out_shape = (bf, jax.ShapeDtypeStruct((n, DA_WIDTH), F32))
    out_specs = (rows, rows)
    if with_kv:
        out_shape += (bf, jax.ShapeDtypeStruct((n // SEQ_TILE, DA_WIDTH, SEQ_TILE), BF16))
        out_specs += (rows, pl.BlockSpec((tm // SEQ_TILE, DA_WIDTH, SEQ_TILE), lambda i: (i, 0, 0)))
    return pl.pallas_call(
        functools.partial(_da_prep_body, tm=tm),
        out_shape=out_shape,
        grid=(n // tm,),
        in_specs=[pl.BlockSpec((tm, DA_COLS), lambda i: (i, 0)),
                  pl.BlockSpec((1, DA_WIDTH), lambda i: (0, 0)),
                  pl.BlockSpec((1, DA_WIDTH), lambda i: (0, 0)),
                  pl.BlockSpec((RW_WIDTH, RW_WIDTH), lambda i: (0, 0))],
        out_specs=out_specs,
        compiler_params=_cparams(("arbitrary",)),
        name="da_prep",
    )(pda, tile8(qg), tile8(kg), e)


def _bucket_starts():
    max_exact = N_BUCKETS // 2
    n = np.arange(1, MAX_DISTANCE + 1)
    val = (np.log(n.astype(np.float32) / np.float32(max_exact))
           / np.float32(math.log(MAX_DISTANCE / max_exact)) * np.float32(N_BUCKETS - max_exact))
    inner = (n > max_exact) & (n < MAX_DISTANCE)
    assert np.all(np.abs(val[inner] - np.round(val[inner])) > 1e-3)
    bucket = np.where(n < max_exact, n, np.minimum(max_exact + val.astype(np.int32), N_BUCKETS - 1))
    return [int(n[np.argmax(bucket >= b)]) for b in range(1, N_BUCKETS)]


def _bias_tiles(n, rel_ref):
    accs = [jnp.full(n.shape, rel_ref[0, h], F32) for h in range(DA_HEADS)]
    for b, start in enumerate(_bucket_starts(), start=1):
        in_or_past = n >= start
        accs = [jnp.where(in_or_past, rel_ref[b, h], acc) for h, acc in enumerate(accs)]
    return [jnp.where(n >= 0, acc, NEG_INF) for acc in accs]


def _lambda(lam_ref, layer):
    lq1 = lam_ref[0:1, :]
    lk1 = lam_ref[1:2, :]
    lq2 = lam_ref[2:3, :]
    lk2 = lam_ref[3:4, :]
    li = 0.8 - 0.6 * math.exp(-0.3 * layer)
    lam = (jnp.exp(jnp.sum(lq1 * lk1, axis=1, keepdims=True))
           - jnp.exp(jnp.sum(lq2 * lk2, axis=1, keepdims=True)) + li)
    return lam, li


def _softmax_update(carries, q_sts, kbs, vts, biases):
    n = len(q_sts)
    score = lambda h: (lax.dot_general(kbs[h], q_sts[h], _QK_DIMS, preferred_element_type=F32)
                       + biases[h])
    out = []
    s_next = score(0)
    for h in range(n):
        s = s_next
        if h + 1 < n:
            s_next = score(h + 1)
        m, l, acc = carries[h]
        m_new = jnp.maximum(m, jnp.max(s, axis=0, keepdims=True))
        alpha = jnp.exp(m - m_new)
        p = jnp.exp(s - m_new)
        l = alpha * l + jnp.sum(p, axis=0, keepdims=True)
        acc = alpha * acc + jnp.dot(vts[h], p.astype(BF16), preferred_element_type=F32)
        out.append((m_new, l, acc))
    return tuple(out)


def _head_out(o_st, z, lam, li, sg_ref, tq):
    o = o_st[:tq] - lam * o_st[tq:]
    ms = jnp.mean(o * o, axis=1, keepdims=True)
    return o * lax.rsqrt(ms + NORM_EPS) * sg_ref[...] * (1.0 - li) * (z * jax.nn.sigmoid(z))


def _split_maps(qh):
    lane = lax.broadcasted_iota(jnp.int32, qh.shape, 1)
    zero = jnp.zeros_like(qh)
    return jnp.where(lane < DA_QK, qh, zero), jnp.where(lane >= DA_QK, qh, zero)


_QK_DIMS = (((1,), (1,)), ((), ()))


def _attn_prompt_body(q_ref, k_ref, vt_ref, z_ref, rel_ref, lam_ref, sg_ref, o_ref, bias_ref,
                      *, tq, layer):
    qi = pl.program_id(1)

    @pl.when((pl.program_id(0) == 0) & (qi == 0))
    def _():
        d = (lax.broadcasted_iota(jnp.int32, (tq, tq), 1)
             - lax.broadcasted_iota(jnp.int32, (tq, tq), 0))
        for blk in range(2):
            tiles = _bias_tiles(d + blk * tq, rel_ref)
            for h in range(DA_HEADS):
                bias_ref[blk * DA_HEADS + h] = tiles[h]

    lam, li = _lambda(lam_ref, layer)
    n_far = jnp.maximum(qi - 1, 0)
    heads = range(DA_HEADS)
    sls = [slice(h * DA_V, (h + 1) * DA_V) for h in heads]
    q_sts = [jnp.concatenate(_split_maps(q_ref[:, sl]), axis=0) for sl in sls]
    far_bias = [rel_ref[N_BUCKETS - 1, h] for h in heads]

    def attend(carries, kblk, nblk, biases):
        off = pl.multiple_of(kblk * tq, tq)
        vts = [jnp.concatenate([vt_ref[kblk + i, sl, :] for i in range(nblk)], axis=1)
               for sl in sls]
        return _softmax_update(carries, q_sts, [k_ref[pl.ds(off, nblk * tq), sl] for sl in sls],
                               vts, biases)

    init = (jnp.full((1, 2 * tq), NEG_INF, F32), jnp.zeros((1, 2 * tq), F32),
            jnp.zeros((DA_V, 2 * tq), F32))
    carries = lax.fori_loop(0, n_far >> 1, lambda j, c: attend(c, 2 * j, 2, far_bias),
                            (init,) * DA_HEADS)
    carries = lax.cond((n_far & 1) == 1, lambda c: attend(c, n_far - 1, 1, far_bias),
                       lambda c: c, carries)
    first = qi == 0
    near = []
    for h in heads:
        top = jnp.where(first, bias_ref[h], bias_ref[DA_HEADS + h])
        bottom = bias_ref[h] + jnp.where(first, NEG_INF, 0.0)
        tile = jnp.concatenate([top, bottom], axis=0)
        near.append(jnp.concatenate([tile, tile], axis=1))
    carries = attend(carries, n_far, 2, near)
    for h in heads:
        _, l, acc = carries[h]
        o_t = acc / l
        o_st = jnp.concatenate([o_t[:, :tq].T, o_t[:, tq:].T], axis=0)
        o_ref[:, sls[h]] = _head_out(o_st, z_ref[:, sls[h]], lam, li, sg_ref, tq)


def _attn_prompt(qn, kb, vt, pda, rel_bias, lam, sg, layer, tq):
    b, t, _ = qn.shape
    nblk = t // tq
    return pl.pallas_call(
        functools.partial(_attn_prompt_body, tq=tq, layer=layer),
        out_shape=jax.ShapeDtypeStruct((b, t, DA_WIDTH), F32),
        grid=(b, nblk),
        in_specs=[pl.BlockSpec((None, tq, DA_WIDTH), lambda i, q: (i, q, 0)),
                  pl.BlockSpec((None, t, DA_WIDTH), lambda i, q: (i, 0, 0)),
                  pl.BlockSpec((nblk, DA_WIDTH, tq), lambda i, q: (i, 0, 0)),
                  pl.BlockSpec((None, tq, DA_WIDTH), lambda i, q: (i, q, 3)),
                  pl.BlockSpec(memory_space=pltpu.SMEM),
                  pl.BlockSpec((4, DA_QK), lambda i, q: (0, 0)),
                  pl.BlockSpec((1, DA_V), lambda i, q: (0, 0))],
        out_specs=pl.BlockSpec((None, tq, DA_WIDTH), lambda i, q: (i, q, 0)),
        scratch_shapes=[pltpu.VMEM((2 * DA_HEADS, tq, tq), F32)],
        compiler_params=_cparams(("arbitrary", "arbitrary")),
        name="attn_prompt",
    )(qn, kb, vt, pda, rel_bias, lam, sg.reshape(1, DA_V))


def _attn_sample_body(pt_ref, q_ref, *refs, tq, npg, layer):
    kc = refs[:npg]
    vc = refs[npg:2 * npg]
    kn_ref, vn_ref, z_ref, rel_ref, lam_ref, sg_ref, o_ref, bias_ref = refs[2 * npg:]
    past = npg * PAGE_SIZE
    ncol = past + PAGE_SIZE

    @pl.when(pl.program_id(0) == 0)
    def _():
        t_idx = lax.rem(lax.broadcasted_iota(jnp.int32, (2 * tq, ncol), 0), tq)
        col = lax.broadcasted_iota(jnp.int32, (2 * tq, ncol), 1)
        tiles = _bias_tiles(past + t_idx - col, rel_ref)
        for h in range(DA_HEADS):
            bias_ref[h] = tiles[h]

    lam, li = _lambda(lam_ref, layer)
    pad = jnp.zeros((PAGE_SIZE - tq, DA_V), F32)
    sls = [slice(h * DA_V, (h + 1) * DA_V) for h in range(DA_HEADS)]

    def gather(h, pages, new_ref):
        rows_h = pl.ds(h, PAGE_SIZE, stride=DA_HEADS)
        page_h = lambda ref: ref.reshape(PAGE_SIZE * DA_HEADS, DA_V)[rows_h, :]
        return jnp.concatenate([page_h(pages[j]) for j in range(npg)] + [new_ref[:, sls[h]], pad],
                               axis=0).astype(BF16)

    def score(h):
        q_st = jnp.concatenate(_split_maps(q_ref[:, sls[h]]), axis=0)
        return lax.dot_general(q_st, gather(h, kc, kn_ref), _QK_DIMS,
                               preferred_element_type=F32) + bias_ref[h]

    outs = []
    s_next = score(0)
    for h in range(DA_HEADS):
        s = s_next
        if h + 1 < DA_HEADS:
            s_next = score(h + 1)
        m = jnp.max(s, axis=1, keepdims=True)
        p = jnp.exp(s - m)
        l = jnp.sum(p, axis=1, keepdims=True)
        o = jnp.dot(p.astype(BF16), gather(h, vc, vn_ref), preferred_element_type=F32) / l
        outs.append(_head_out(o, z_ref[:, sls[h]], lam, li, sg_ref, tq))
    o_ref[...] = jnp.concatenate(outs, axis=1)


def _attn_sample(qn, kn, pda, cache_k, cache_v, page_table, rel_bias, lam, sg, layer):
    b, tq, _ = qn.shape
    npg = page_table.shape[1]
    page_specs = [pl.BlockSpec((None, None, PAGE_SIZE, DA_HEADS, DA_V),
                               lambda i, pt, j=j: (layer, pt[i, j], 0, 0, 0)) for j in range(npg)]
    seq_spec = lambda col: pl.BlockSpec((None, tq, DA_WIDTH), lambda i, pt: (i, 0, col))
    grid_spec = pltpu.PrefetchScalarGridSpec(
        num_scalar_prefetch=1,
        grid=(b,),
        in_specs=[seq_spec(0)] + page_specs + page_specs
                 + [seq_spec(0), seq_spec(2), seq_spec(3),
                    pl.BlockSpec(memory_space=pltpu.SMEM),
                    pl.BlockSpec((4, DA_QK), lambda i, pt: (0, 0)),
                    pl.BlockSpec((1, DA_V), lambda i, pt: (0, 0))],
        out_specs=pl.BlockSpec((None, tq, DA_WIDTH), lambda i, pt: (i, 0, 0)),
        scratch_shapes=[pltpu.VMEM((DA_HEADS, 2 * tq, (npg + 1) * PAGE_SIZE), F32)],
    )
    return pl.pallas_call(
        functools.partial(_attn_sample_body, tq=tq, npg=npg, layer=layer),
        out_shape=jax.ShapeDtypeStruct((b, tq, DA_WIDTH), F32),
        grid_spec=grid_spec,
        compiler_params=_cparams(("arbitrary",)),
        name="attn_sample",
    )(page_table, qn, *([cache_k] * npg), *([cache_v] * npg), kn, pda, pda, rel_bias, lam,
      sg.reshape(1, DA_V))


def _out_proj_body(x_ref, y_ref, sz_ref, bz_ref, yb_ref, g_ref, lg_ref, lb_ref, e_ref,
                   wa_ref, wb_ref, wo_ref, o_ref):
    y = y_ref[...]
    mean = _head_sum(y, e_ref) * (1.0 / RW_HEAD)
    yc = y - mean
    var = _head_sum(yc * yc, e_ref) * (1.0 / RW_HEAD)
    yn = yc * lax.rsqrt(var + GN_EPS) * lg_ref[...] + lb_ref[...]
    ya = yn * sz_ref[...] + bz_ref[...]
    ga = jax.nn.sigmoid(g_ref[:, 0:D_MODEL])
    gb = jax.nn.sigmoid(g_ref[:, D_MODEL:2 * D_MODEL])
    m = (ga * jnp.dot(ya.astype(BF16), wa_ref[...], preferred_element_type=F32)
         + gb * jnp.dot(yb_ref[...].astype(BF16), wb_ref[...], preferred_element_type=F32))
    o_ref[...] = x_ref[...] + jnp.dot(m.astype(BF16), wo_ref[...], preferred_element_type=F32)


def _out_proj(x2d, y, sz, bz, yb, pg, lnx_g, lnx_b, e, wa, wb, wo, tm):
    n = x2d.shape[0]
    rows = lambda c: pl.BlockSpec((tm, c), lambda i: (i, 0))
    full = lambda r, c: pl.BlockSpec((r, c), lambda i: (0, 0))
    return pl.pallas_call(
        _out_proj_body,
        out_shape=jax.ShapeDtypeStruct((n, D_MODEL), F32),
        grid=(n // tm,),
        in_specs=[rows(D_MODEL), rows(RW_WIDTH), rows(RW_WIDTH), rows(RW_WIDTH), rows(DA_WIDTH),
                  rows(G_COLS), full(1, RW_WIDTH), full(1, RW_WIDTH), full(RW_WIDTH, RW_WIDTH),
                  full(RW_WIDTH, D_MODEL), full(DA_WIDTH, D_MODEL), full(D_MODEL, D_MODEL)],
        out_specs=rows(D_MODEL),
        compiler_params=_cparams(("arbitrary",)),
        name="out_proj",
    )(x2d, y, sz, bz, yb, pg, lnx_g.reshape(1, -1), lnx_b.reshape(1, -1), e, wa, wb, wo)


def _trunk(x, t_valid, tm, tt, shift0, state0, cache_k, cache_v, page_table, prm):
    b, t, _ = x.shape
    n = b * t
    e = (jnp.arange(RW_WIDTH)[:, None] // RW_HEAD == jnp.arange(RW_WIDTH)[None, :] // RW_HEAD)
    e = e.astype(BF16)
    new_k, new_v, new_s, new_shift = [], [], [], []
    for l in range(2):
        w_in = prm["w_in"][l]
        w_rw = w_in[:, :RW_COLS].astype(BF16)
        x2d = x.reshape(n, D_MODEL)
        tp = 2 * tm if n % (2 * tm) == 0 else tm
        p_rw = _in_proj(x2d, prm["norm_g"][l], w_rw, tp)
        pda = _in_proj(x2d, prm["norm_g"][l], w_in[:, RW_COLS:RW_COLS + DA_COLS].astype(BF16), tp)
        pg = _in_proj(x2d, prm["norm_g"][l], w_in[:, RW_COLS + DA_COLS:].astype(BF16), tp)
        new_shift.append(_norm_rows(x[:, t_valid - 1], prm["norm_g"][l]))

        if shift0 is None:
            prev = jnp.zeros((b, 1, RW_COLS), F32)
            s_init = jnp.zeros((b, RW_HEADS, RW_HEAD, RW_HEAD), F32)
        else:
            prev = _in_proj(shift0[l], prm["norm_g"][l], w_rw, shift0[l].shape[0],
                            normalize=False).reshape(b, 1, RW_COLS)
            s_init = state0[l]
        y, sz, bz, s_fin = _rwkv_mix(
            p_rw.reshape(b, t, RW_COLS), prev, s_init, prm["mu_shift"][l], prm["w0"][l],
            prm["w2"][l], prm["a0"][l], prm["a2"][l], prm["k_k"][l], prm["k_a"][l], prm["r_k"][l],
            e, t_valid)

        qn, kn, *kv = _da_prep(pda, prm["q_norm_g"][l], prm["k_norm_g"][l], e, tm, cache_k is None)
        lam = jnp.stack([prm["lam_q1"][l], prm["lam_k1"][l], prm["lam_q2"][l], prm["lam_k2"][l]])
        pda3 = pda.reshape(b, t, DA_COLS)
        qn3 = qn.reshape(b, t, DA_WIDTH)
        kn3 = kn.reshape(b, t, DA_WIDTH)
        if cache_k is None:
            yb = _attn_prompt(qn3, kv[0].reshape(b, t, DA_WIDTH), kv[1], pda3, prm["rel_bias"], lam,
                              prm["subln_g"][l], l, tt)
        else:
            yb = _attn_sample(qn3, kn3, pda3, cache_k, cache_v, page_table, prm["rel_bias"], lam,
                              prm["subln_g"][l], l)

        x = _out_proj(x2d, y.reshape(n, RW_WIDTH), sz.reshape(n, RW_WIDTH),
                      bz.reshape(n, RW_WIDTH), yb.reshape(n, DA_WIDTH), pg, prm["lnx_g"][l],
                      prm["lnx_b"][l], e, prm["w_a_out"][l].astype(BF16),
                      prm["w_b_out"][l].astype(BF16), prm["w_o"][l].astype(BF16),
                      tm).reshape(b, t, D_MODEL)
        new_k.append(kn3[:, :t_valid].reshape(b, t_valid, DA_HEADS, DA_V))
        new_v.append(pda3[:, :t_valid, 2 * DA_WIDTH:3 * DA_WIDTH].reshape(b, t_valid, DA_HEADS, DA_V))
        new_s.append(s_fin)
    return x, jnp.stack(new_k), jnp.stack(new_v), jnp.stack(new_s), jnp.stack(new_shift)


def kernel(x_prompt, x_sample, cache_k, cache_v, page_table, state_rwkv, state_shift, meta_tokens,
           rel_bias, norm_g, w_in, mu_shift, w0, w2, a0, a2, k_k, k_a, r_k, lnx_g, lnx_b,
           q_norm_g, k_norm_g, lam_q1, lam_k1, lam_q2, lam_k2, subln_g, w_a_out, w_b_out, w_o):
    prm = dict(rel_bias=rel_bias, norm_g=norm_g, w_in=w_in, mu_shift=mu_shift, w0=w0, w2=w2,
               a0=a0, a2=a2, k_k=k_k, k_a=k_a, r_k=r_k.reshape(2, RW_WIDTH), lnx_g=lnx_g,
               lnx_b=lnx_b, q_norm_g=q_norm_g, k_norm_g=k_norm_g, lam_q1=lam_q1, lam_k1=lam_k1,
               lam_q2=lam_q2, lam_k2=lam_k2, subln_g=subln_g, w_a_out=w_a_out, w_b_out=w_b_out,
               w_o=w_o)
    bp, seq, _ = x_prompt.shape
    t_valid = seq + N_META
    t_pad = -(-t_valid // SEQ_TILE) * SEQ_TILE
    meta = jnp.broadcast_to(meta_tokens[None], (bp, N_META, D_MODEL))
    xp = jnp.concatenate([meta, x_prompt, jnp.zeros((bp, t_pad - t_valid, D_MODEL), F32)], axis=1)
    yp, kp, vp, sp, hp = _trunk(xp, t_valid, 512, SEQ_TILE, None, None, None, None, None, prm)

    ts = x_sample.shape[1]
    ys, k_s, v_s, s_s, h_s = _trunk(x_sample, ts, 512, ts, state_shift, state_rwkv, cache_k,
                                    cache_v, page_table, prm)
    return (yp[:, N_META:t_valid], ys, kp, vp, sp, hp, k_s, v_s, s_s, h_s)
```
